```python
import jax
import jax.numpy as jnp
from jax import lax
import numpy as np

D_MODEL = 1024
BATCH = 8
SEQ = 2048
DEPTH = 1
DEC_BATCH = 16
DEC_SEQ = 64
PAST_LEN = 4096

CHUNK = 64
N_META = 16
N_HEADS = 8
N_KV = 2
Q_PER_KV = N_HEADS // N_KV
HEAD_DIM = 64
D_ATTN = N_HEADS * HEAD_DIM
D_KV = N_KV * HEAD_DIM
WINDOW = 128
WIN_CHUNKS = WINDOW // CHUNK
ROPE_DIM = HEAD_DIM // 4
ROPE_THETA = 500000.0
D_RNN = D_MODEL
N_RNN_BLOCKS = 8
RNN_BLOCK = D_RNN // N_RNN_BLOCKS
CONV_W = 4
LRU_C = 8.0
N_GROUPS = 4
EXPERTS_PER_GROUP = 4
N_EXPERTS = N_GROUPS * EXPERTS_PER_GROUP
TOP_K_INNER = 2
D_EXPERT = 512
ALPHA = (2 * DEPTH) ** 0.25
BETA = (8 * DEPTH) ** -0.25
LN_EPS = 1e-5
D_IN = D_ATTN + 2 * D_KV + 2 * D_RNN + 2 * D_MODEL
IN_SPLITS = (D_ATTN, D_ATTN + D_KV, D_ATTN + 2 * D_KV, D_ATTN + 2 * D_KV + D_RNN, D_ATTN + 2 * D_KV + 2 * D_RNN)

kernel_name = 'hybrid_swa_sink_rglru_hmoe_stream_step'


def layer_norm(x, g, b):
    xf = x.astype(jnp.float32)
    mu = jnp.mean(xf, axis=-1, keepdims=True)
    var = jnp.mean(jnp.square(xf - mu), axis=-1, keepdims=True)
    y = (xf - mu) * lax.rsqrt(var + LN_EPS) * g.astype(jnp.float32) + b.astype(jnp.float32)
    return y.astype(x.dtype)


def partial_rope(x, pos):
    half = ROPE_DIM // 2
    inv_freq = ROPE_THETA ** (-jnp.arange(half, dtype=jnp.float32) / half)
    ang = pos.astype(jnp.float32)[:, None] * inv_freq[None, :]
    cos = jnp.cos(ang)[:, None, :]
    sin = jnp.sin(ang)[:, None, :]
    xr = x[..., :ROPE_DIM].astype(jnp.float32)
    x1, x2 = xr[..., :half], xr[..., half:]
    rot = jnp.concatenate([x1 * cos - x2 * sin, x2 * cos + x1 * sin], axis=-1)
    return jnp.concatenate([rot.astype(x.dtype), x[..., ROPE_DIM:]], axis=-1)


def project_in(h, w_in, b_gate, pos):
    B, T, _ = h.shape
    z = h @ w_in
    q, k, v, xr, gr, gl = jnp.split(z, IN_SPLITS, axis=-1)
    q = partial_rope(q.reshape(B, T, N_HEADS, HEAD_DIM), pos)
    k = partial_rope(k.reshape(B, T, N_KV, HEAD_DIM), pos)
    v = v.reshape(B, T, N_KV, HEAD_DIM)
    return q, k, v, xr, gr, gl + b_gate


def sink_attention(q, k, v, mask, sinks):
    B, N, Q = q.shape[:3]
    qg = q.reshape(B, N, Q, N_KV, Q_PER_KV, HEAD_DIM)
    s = jnp.einsum('bnqkgd,bnlkd->bnkgql', qg, k, preferred_element_type=jnp.float32) * (HEAD_DIM ** -0.5)
    s = jnp.where(mask[None, :, None, None], s, -jnp.inf)
    sink = sinks.astype(jnp.float32).reshape(1, 1, N_KV, Q_PER_KV, 1, 1)
    m = jnp.maximum(jnp.max(s, axis=-1, keepdims=True), sink)
    p = jnp.exp(s - m)
    p = p / (jnp.sum(p, axis=-1, keepdims=True) + jnp.exp(sink - m))
    o = jnp.einsum('bnkgql,bnlkd->bnqkgd', p.astype(v.dtype), v)
    return o.reshape(B, N * Q, D_ATTN)


def prompt_attention(q, k, v, sinks):
    B, T = q.shape[:2]
    S = T - N_META
    NC = S // CHUNK
    band = (WIN_CHUNKS + 1) * CHUNK
    o_meta = sink_attention(q[:, None, :N_META], k[:, None, :N_META], v[:, None, :N_META],
                            jnp.ones((1, N_META, N_META), bool), sinks)
    qc = q[:, N_META:].reshape(B, NC, CHUNK, N_HEADS, HEAD_DIM)

    def band_rows(x):
        xc = x[:, N_META:].reshape(B, NC, CHUNK, N_KV, HEAD_DIM)
        xp = jnp.pad(xc, ((0, 0), (WIN_CHUNKS, 0), (0, 0), (0, 0), (0, 0)))
        rows = jnp.concatenate([xp[:, j:j + NC] for j in range(WIN_CHUNKS + 1)], axis=2)
        meta = jnp.broadcast_to(x[:, None, :N_META], (B, NC, N_META, N_KV, HEAD_DIM))
        return jnp.concatenate([meta, rows], axis=2)

    kb, vb = band_rows(k), band_rows(v)
    key_chunk = jnp.arange(NC)[:, None] - WIN_CHUNKS + jnp.arange(band)[None, :] // CHUNK
    mask = jnp.concatenate([jnp.ones((NC, N_META), bool), key_chunk >= 0], axis=1)
    mask = jnp.broadcast_to(mask[:, None, :], (NC, CHUNK, N_META + band))
    o_real = sink_attention(qc, kb, vb, mask, sinks)
    return jnp.concatenate([o_meta, o_real], axis=1)


def rglru_branch(xr, gr, conv_buf, h0, conv_w, conv_b, w_a, b_a, w_x, b_x, lam):
    B, T, _ = xr.shape
    xpad = jnp.concatenate([conv_buf.astype(xr.dtype), xr], axis=1)
    xc = sum(xpad[:, j:j + T] * conv_w[j] for j in range(CONV_W)) + conv_b
    new_buf = xpad[:, -(CONV_W - 1):]
    xb = xc.reshape(B, T, N_RNN_BLOCKS, RNN_BLOCK)
    r = jax.nn.sigmoid((jnp.einsum('btnc,nce->btne', xb, w_a).reshape(B, T, D_RNN) + b_a).astype(jnp.float32))
    i = jax.nn.sigmoid((jnp.einsum('btnc,nce->btne', xb, w_x).reshape(B, T, D_RNN) + b_x).astype(jnp.float32))
    log_a = -LRU_C * r * jax.nn.softplus(-lam.astype(jnp.float32))
    a = jnp.exp(log_a)
    u = jnp.sqrt(-jnp.expm1(2.0 * log_a)) * i * xc.astype(jnp.float32)

    def step(h, au):
        a_t, u_t = au
        h = a_t * h + u_t
        return h, h

    h_last, hs = lax.scan(step, h0.astype(jnp.float32), (jnp.swapaxes(a, 0, 1), jnp.swapaxes(u, 0, 1)))
    y = jnp.swapaxes(hs, 0, 1) * jax.nn.gelu(gr.astype(jnp.float32))
    return y.astype(xr.dtype), new_buf, h_last.astype(xr.dtype)


def merge_branches(attn, rnn, gl, w_pa, w_pr, w_o):
    g = jax.nn.sigmoid(gl)
    g_attn, g_rnn = g[..., :D_MODEL], g[..., D_MODEL:]
    return (g_attn * (attn @ w_pa) + g_rnn * (rnn @ w_pr)) @ w_o


def hier_moe(x, w_group, b_group, w_router, b_router, w_gate, w_up, w_down):
    shp = x.shape
    xt = x.reshape(-1, D_MODEL)
    n = xt.shape[0]
    rows = jnp.arange(n)
    g_logits = (xt @ w_group).astype(jnp.float32) + b_group.astype(jnp.float32)
    g_prob = jax.nn.softmax(g_logits, axis=-1)
    g_idx = jnp.argmax(g_logits, axis=-1)
    g_w = g_prob[rows, g_idx][:, None]
    e_logits = jnp.einsum('nd,dge->nge', xt, w_router).astype(jnp.float32) + b_router.astype(jnp.float32)
    e_sel = e_logits[rows, g_idx]
    top_v, top_i = lax.top_k(e_sel, TOP_K_INNER)
    top_w = jax.nn.softmax(top_v, axis=-1) * g_w
    e_idx = g_idx[:, None] * EXPERTS_PER_GROUP + top_i
    combine = jnp.sum(jax.nn.one_hot(e_idx, N_EXPERTS, dtype=jnp.float32) * top_w[..., None], axis=1)
    y = jnp.zeros((n, D_MODEL), jnp.float32)
    for e in range(N_EXPERTS):
        h = jax.nn.silu(xt @ w_gate[e]) * (xt @ w_up[e])
        y = y + combine[:, e:e + 1] * (h @ w_down[e]).astype(jnp.float32)
    return y.astype(x.dtype).reshape(shp)


def setup_inputs(seed: int = 0) -> dict:
    key = jax.random.key(seed)
    keys = jax.random.split(key, 40)

    def nrm(i, shape, scale):
        return scale * jax.random.normal(keys[i], shape, jnp.float32)

    win_rows = min(WINDOW, PAST_LEN)
    a0 = jax.random.uniform(keys[20], (DEPTH, D_RNN), jnp.float32, 0.9, 0.999)
    s0 = a0 ** (1.0 / LRU_C)
    return {
        'x_prompt': nrm(0, (BATCH, SEQ, D_MODEL), 1.0),
        'x_sample': nrm(1, (DEC_BATCH, DEC_SEQ, D_MODEL), 1.0),
        'cache_meta_k': nrm(2, (DEPTH, DEC_BATCH, N_META, N_KV, HEAD_DIM), 1.0),
        'cache_meta_v': nrm(3, (DEPTH, DEC_BATCH, N_META, N_KV, HEAD_DIM), 1.0),
        'cache_win_k': nrm(4, (DEPTH, DEC_BATCH, win_rows, N_KV, HEAD_DIM), 1.0),
        'cache_win_v': nrm(5, (DEPTH, DEC_BATCH, win_rows, N_KV, HEAD_DIM), 1.0),
        'state_conv': nrm(6, (DEPTH, DEC_BATCH, CONV_W - 1, D_RNN), 1.0),
        'state_h': nrm(7, (DEPTH, DEC_BATCH, D_RNN), 0.5),
        'meta_tokens': nrm(8, (N_META, D_MODEL), 1.0),
        'ln_in_g': 1.0 + nrm(9, (D_MODEL,), 0.02),
        'ln_in_b': nrm(10, (D_MODEL,), 0.02),
        'w_in': nrm(11, (DEPTH, D_MODEL, D_IN), D_MODEL ** -0.5),
        'b_gate': nrm(12, (DEPTH, 2 * D_MODEL), 0.02),
        'attn_sinks': nrm(13, (DEPTH, N_HEADS), 0.5),
        'conv_w': nrm(14, (DEPTH, CONV_W, D_RNN), CONV_W ** -0.5),
        'conv_b': nrm(15, (DEPTH, D_RNN), 0.02),
        'w_rg_a': nrm(16, (DEPTH, N_RNN_BLOCKS, RNN_BLOCK, RNN_BLOCK), RNN_BLOCK ** -0.5),
        'b_rg_a': nrm(17, (DEPTH, D_RNN), 0.02),
        'w_rg_x': nrm(18, (DEPTH, N_RNN_BLOCKS, RNN_BLOCK, RNN_BLOCK), RNN_BLOCK ** -0.5),
        'b_rg_x': nrm(19, (DEPTH, D_RNN), 0.02),
        'lru_lambda': jnp.log(s0) - jnp.log1p(-s0),
        'w_branch_attn': nrm(21, (DEPTH, D_ATTN, D_MODEL), BETA * D_ATTN ** -0.5),
        'w_branch_rnn': nrm(22, (DEPTH, D_RNN, D_MODEL), BETA * D_RNN ** -0.5),
        'w_out': nrm(23, (DEPTH, D_MODEL, D_MODEL), BETA * D_MODEL ** -0.5),
        'ln1_g': 1.0 + nrm(24, (DEPTH, D_MODEL), 0.02),
        'ln1_b': nrm(25, (DEPTH, D_MODEL), 0.02),
        'w_group': nrm(26, (DEPTH, D_MODEL, N_GROUPS), D_MODEL ** -0.5),
        'b_group': nrm(27, (DEPTH, N_GROUPS), 0.01),
        'w_router': nrm(28, (DEPTH, D_MODEL, N_GROUPS, EXPERTS_PER_GROUP), D_MODEL ** -0.5),
        'b_router': nrm(29, (DEPTH, N_GROUPS, EXPERTS_PER_GROUP), 0.01),
        'w_gate': nrm(30, (DEPTH, N_EXPERTS, D_MODEL, D_EXPERT), D_MODEL ** -0.5),
        'w_up': nrm(31, (DEPTH, N_EXPERTS, D_MODEL, D_EXPERT), D_MODEL ** -0.5),
        'w_down': nrm(32, (DEPTH, N_EXPERTS, D_EXPERT, D_MODEL), BETA * D_EXPERT ** -0.5),
        'ln2_g': 1.0 + nrm(33, (DEPTH, D_MODEL), 0.02),
        'ln2_b': nrm(34, (DEPTH, D_MODEL), 0.02),
    }


def reference(x_prompt, x_sample, cache_meta_k, cache_meta_v, cache_win_k, cache_win_v, state_conv, state_h,
              meta_tokens, ln_in_g, ln_in_b, w_in, b_gate, attn_sinks, conv_w, conv_b, w_rg_a, b_rg_a,
              w_rg_x, b_rg_x, lru_lambda, w_branch_attn, w_branch_rnn, w_out, ln1_g, ln1_b, w_group, b_group,
              w_router, b_router, w_gate, w_up, w_down, ln2_g, ln2_b):
    B, S, _ = x_prompt.shape
    DB, TS, _ = x_sample.shape
    T = N_META + S
    meta = jnp.broadcast_to(meta_tokens.astype(x_prompt.dtype)[None], (B, N_META, D_MODEL))
    z_p = layer_norm(jnp.concatenate([meta, x_prompt], axis=1), ln_in_g, ln_in_b)
    z_s = layer_norm(x_sample, ln_in_g, ln_in_b)
    pos_p = jnp.arange(T)
    pos_s = N_META + PAST_LEN + jnp.arange(TS)
    win_p = min(WINDOW, S)
    mk_p, mv_p, wk_p, wv_p, cv_p, hh_p = [], [], [], [], [], []
    nk_s, nv_s, cv_s, hh_s = [], [], [], []
    for l in range(DEPTH):
        q, k, v, xr, gr, gl = project_in(z_p, w_in[l], b_gate[l], pos_p)
        attn = prompt_attention(q, k, v, attn_sinks[l])
        rnn, conv_new, h_new = rglru_branch(
            xr, gr, jnp.zeros((B, CONV_W - 1, D_RNN), xr.dtype), jnp.zeros((B, D_RNN), jnp.float32),
            conv_w[l], conv_b[l], w_rg_a[l], b_rg_a[l], w_rg_x[l], b_rg_x[l], lru_lambda[l])
        mix = merge_branches(attn, rnn, gl, w_branch_attn[l], w_branch_rnn[l], w_out[l])
        z_p = layer_norm(ALPHA * z_p + mix, ln1_g[l], ln1_b[l])
        z_p = layer_norm(ALPHA * z_p + hier_moe(z_p, w_group[l], b_group[l], w_router[l], b_router[l],
                                                w_gate[l], w_up[l], w_down[l]), ln2_g[l], ln2_b[l])
        mk_p.append(k[:, :N_META])
        mv_p.append(v[:, :N_META])
        wk_p.append(k[:, T - win_p:])
        wv_p.append(v[:, T - win_p:])
        cv_p.append(conv_new)
        hh_p.append(h_new)
        q, k, v, xr, gr, gl = project_in(z_s, w_in[l], b_gate[l], pos_s)
        k_all = jnp.concatenate([cache_meta_k[l].astype(k.dtype), cache_win_k[l].astype(k.dtype), k], axis=1)
        v_all = jnp.concatenate([cache_meta_v[l].astype(v.dtype), cache_win_v[l].astype(v.dtype), v], axis=1)
        attn = sink_attention(q[:, None], k_all[:, None], v_all[:, None],
                              jnp.ones((1, TS, k_all.shape[1]), bool), attn_sinks[l])
        rnn, conv_new, h_new = rglru_branch(
            xr, gr, state_conv[l], state_h[l],
            conv_w[l], conv_b[l], w_rg_a[l], b_rg_a[l], w_rg_x[l], b_rg_x[l], lru_lambda[l])
        mix = merge_branches(attn, rnn, gl, w_branch_attn[l], w_branch_rnn[l], w_out[l])
        z_s = layer_norm(ALPHA * z_s + mix, ln1_g[l], ln1_b[l])
        z_s = layer_norm(ALPHA * z_s + hier_moe(z_s, w_group[l], b_group[l], w_router[l], b_router[l],
                                                w_gate[l], w_up[l], w_down[l]), ln2_g[l], ln2_b[l])
        nk_s.append(k)
        nv_s.append(v)
        cv_s.append(conv_new)
        hh_s.append(h_new.astype(state_h.dtype))
    y_prompt = z_p[:, N_META:]
    y_sample = z_s
    meta_k_p = jnp.stack(mk_p, axis=0)
    meta_v_p = jnp.stack(mv_p, axis=0)
    win_k_p = jnp.stack(wk_p, axis=0)
    win_v_p = jnp.stack(wv_p, axis=0)
    conv_p = jnp.stack(cv_p, axis=0)
    h_p = jnp.stack(hh_p, axis=0)
    new_k_s = jnp.stack(nk_s, axis=0)
    new_v_s = jnp.stack(nv_s, axis=0)
    conv_s = jnp.stack(cv_s, axis=0)
    h_s = jnp.stack(hh_s, axis=0)
    return (y_prompt, y_sample, meta_k_p, meta_v_p, win_k_p, win_v_p, conv_p, h_p, new_k_s, new_v_s, conv_s, h_s)
```

```python
import functools

import numpy as np
import jax
import jax.numpy as jnp
from jax import lax
from jax.experimental import pallas as pl
from jax.experimental.pallas import tpu as pltpu

F32 = jnp.float32
BF16 = jnp.bfloat16

D_MODEL = 1024
N_META = 16
CHUNK = 64
N_HEADS = 8
N_KV = 2
HEAD_DIM = 64
D_ATTN = N_HEADS * HEAD_DIM
D_KV = N_KV * HEAD_DIM
WINDOW = 128
PAST_LEN = 4096
ROPE_DIM = HEAD_DIM // 4
ROPE_THETA = 500000.0
D_RNN = D_MODEL
N_RNN_BLOCKS = 8
RNN_BLOCK = D_RNN // N_RNN_BLOCKS
CONV_W = 4
LRU_C = 8.0
N_GROUPS = 4
EXPERTS_PER_GROUP = 4
D_EXPERT = 512
DEPTH = 1
ALPHA = (2 * DEPTH) ** 0.25
LN_EPS = 1e-5

C_Q, C_K, C_V = 0, D_ATTN, D_ATTN + D_KV
C_XR = D_ATTN + 2 * D_KV
C_GR = C_XR + D_RNN
C_GL = C_GR + D_RNN
D_IN = C_GL + 2 * D_MODEL

LANE = 128
SUBLANE = 8
TM = 256
TS = 256
N_PAIR = 6
N_CLASS = N_GROUPS * N_PAIR
XS_W = D_MODEL + LANE
VMEM_LIMIT = 56 * 1024 * 1024

PAIR_A = np.array([0, 0, 0, 1, 1, 2], np.int32)
PAIR_B = np.array([1, 2, 3, 2, 3, 3], np.int32)


def _params(n_grid_dims=1):
    return pltpu.CompilerParams(
        dimension_semantics=("arbitrary",) * n_grid_dims, vmem_limit_bytes=VMEM_LIMIT)


def _layer_norm(x, g, b):
    mu = jnp.mean(x, axis=-1, keepdims=True)
    xc = x - mu
    var = jnp.mean(xc * xc, axis=-1, keepdims=True)
    return xc * lax.rsqrt(var + LN_EPS) * g + b


def _dot(a, b):
    return jnp.dot(a, b, preferred_element_type=F32)


def _inproj_kernel(x_ref, lng_ref, lnb_ref, win_ref, bgate_ref, rc_ref, rs1_ref, rs2_ref,
                   cw_ref, cb_ref, wrg_ref, ba_ref, bx_ref, lam_ref, iconv_ref, ih_ref,
                   q_ref, k_ref, v_ref, rnn_ref, g_ref, conv_ref, h_ref,
                   xbuf, abuf, ubuf, hcar, *, seg, nseg, tiles_per_stream):
    i = pl.program_id(0)
    zn = _layer_norm(x_ref[...], lng_ref[...], lnb_ref[...]).astype(BF16)

    qkv = _dot(zn, win_ref[:, C_Q:C_XR])
    rc, rs1, rs2 = rc_ref[...], rs1_ref[...], rs2_ref[...]

    def rope(s):
        return s * rc + pltpu.roll(s, 8, 1) * rs1 + pltpu.roll(s, LANE - 8, 1) * rs2

    for j in range(D_ATTN // LANE):
        q_ref[:, j * LANE:(j + 1) * LANE] = (
            rope(qkv[:, j * LANE:(j + 1) * LANE]) * (HEAD_DIM ** -0.5)).astype(BF16)
    k_ref[...] = rope(qkv[:, C_K:C_V])
    v_ref[...] = qkv[:, C_V:C_XR]

    gl = _dot(zn, win_ref[:, C_GL:D_IN]) + bgate_ref[...]
    g_ref[...] = jax.nn.sigmoid(gl).astype(BF16)

    xr = _dot(zn, win_ref[:, C_XR:C_GR])
    gr = _dot(zn, win_ref[:, C_GR:C_GL])

    c8 = LRU_C * jax.nn.softplus(-lam_ref[...])
    row = lax.broadcasted_iota(jnp.int32, (SUBLANE, D_RNN), 0)
    cw = cw_ref[...]

    for s in range(nseg):
        r0 = s * seg

        def load_init():
            xbuf[0:SUBLANE, :] = iconv_ref[s]
            hcar[...] = ih_ref[s]

        if tiles_per_stream == 1:
            load_init()
        else:
            pl.when(i % tiles_per_stream == 0)(load_init)

        xbuf[SUBLANE:SUBLANE + seg, :] = xr[r0:r0 + seg, :]
        xc = (xbuf[SUBLANE:SUBLANE + seg, :] * cw[3:4, :]
              + xbuf[SUBLANE - 1:SUBLANE - 1 + seg, :] * cw[2:3, :]
              + xbuf[SUBLANE - 2:SUBLANE - 2 + seg, :] * cw[1:2, :]
              + xbuf[SUBLANE - 3:SUBLANE - 3 + seg, :] * cw[0:1, :]
              + cb_ref[...])
        tail = xbuf[seg:seg + SUBLANE, :]
        conv_ref[s] = tail
        xbuf[0:SUBLANE, :] = tail

        for n in range(N_RNN_BLOCKS):
            sl = slice(n * RNN_BLOCK, (n + 1) * RNN_BLOCK)
            xcn = xc[:, sl]
            ri = _dot(xcn.astype(BF16), wrg_ref[n])
            r = jax.nn.sigmoid(ri[:, :RNN_BLOCK] + ba_ref[:, sl])
            ig = jax.nn.sigmoid(ri[:, RNN_BLOCK:] + bx_ref[:, sl])
            log_a = -c8[:, sl] * r
            a = jnp.exp(log_a)
            mult = jnp.sqrt(-jnp.tanh(log_a) * (a * a + 1.0))
            abuf[0:seg, sl] = a
            ubuf[0:seg, sl] = mult * ig * xcn

        def group(gi, h):
            o = pl.multiple_of(gi * SUBLANE, SUBLANE)
            a = abuf[pl.ds(o, SUBLANE), :]
            u = ubuf[pl.ds(o, SUBLANE), :]
            for sh in (1, 2, 4):
                keep = row >= sh
                u = u + a * jnp.where(keep, pltpu.roll(u, sh, 0), 0.0)
                a = a * jnp.where(keep, pltpu.roll(a, sh, 0), 1.0)
            hg = a * h + u
            ubuf[pl.ds(o, SUBLANE), :] = hg
            return hg[SUBLANE - 1:SUBLANE, :]

        h_last = lax.fori_loop(0, seg // SUBLANE, group, hcar[0:1, :])
        h_b = jnp.broadcast_to(h_last, (SUBLANE, D_RNN))
        hcar[...] = h_b
        h_ref[s] = h_b
        rnn_ref[r0:r0 + seg, :] = (ubuf[0:seg, :] * jax.nn.gelu(gr[r0:r0 + seg, :])).astype(BF16)


def _inproj_call(x2d, rope_tabs, iconv, ih, wts, *, tm, seg, tiles_per_stream, rope_tiles, name):
    n = x2d.shape[0]
    nt = n // tm
    nseg = tm // seg
    n_streams = n // (seg * tiles_per_stream) if tiles_per_stream > 1 else n // seg
    shared_init = iconv.shape[0] == 1

    def full(a):
        return pl.BlockSpec(a.shape, lambda i: (0,) * a.ndim)

    def rows(w):
        return pl.BlockSpec((tm, w), lambda i: (i, 0))

    rope_spec = pl.BlockSpec((tm, LANE), lambda i: (i % rope_tiles, 0))
    if shared_init:
        init_spec = pl.BlockSpec((1, SUBLANE, D_RNN), lambda i: (0, 0, 0))
    else:
        init_spec = pl.BlockSpec((nseg, SUBLANE, D_RNN), lambda i: (i, 0, 0))
    state_spec = pl.BlockSpec((nseg, SUBLANE, D_RNN), lambda i: (i // tiles_per_stream, 0, 0))

    (lng, lnb, win, bgate, cw, cb, wrg, ba, bx, lam) = wts
    kern = functools.partial(_inproj_kernel, seg=seg, nseg=nseg, tiles_per_stream=tiles_per_stream)
    return pl.pallas_call(
        kern,
        grid=(nt,),
        in_specs=[rows(D_MODEL), full(lng), full(lnb), full(win), full(bgate),
                  rope_spec, rope_spec, rope_spec,
                  full(cw), full(cb), full(wrg), full(ba), full(bx), full(lam),
                  init_spec, init_spec],
        out_specs=[rows(D_ATTN), rows(D_KV), rows(D_KV), rows(D_RNN), rows(2 * D_MODEL),
                   state_spec, state_spec],
        out_shape=[jax.ShapeDtypeStruct((n, D_ATTN), BF16),
                   jax.ShapeDtypeStruct((n, D_KV), F32),
                   jax.ShapeDtypeStruct((n, D_KV), F32),
                   jax.ShapeDtypeStruct((n, D_RNN), BF16),
                   jax.ShapeDtypeStruct((n, 2 * D_MODEL), BF16),
                   jax.ShapeDtypeStruct((n_streams, SUBLANE, D_RNN), F32),
                   jax.ShapeDtypeStruct((n_streams, SUBLANE, D_RNN), F32)],
        scratch_shapes=[pltpu.VMEM((SUBLANE + seg, D_RNN), F32),
                        pltpu.VMEM((seg, D_RNN), F32),
                        pltpu.VMEM((seg, D_RNN), F32),
                        pltpu.VMEM((SUBLANE, D_RNN), F32)],
        compiler_params=_params(),
        name=name,
    )(x2d, lng, lnb, win, bgate, *rope_tabs, cw, cb, wrg, ba, bx, lam, iconv, ih)


N_KEYS = N_META + WINDOW + CHUNK
KEY_PAD = 2 * LANE


def _attn_kernel(sink_ref, q_ref, km_ref, vm_ref, kp_ref, vp_ref, ko_ref, vo_ref, o_ref,
                 *, nchunk, tiles_per_stream, mask_first):
    i = pl.program_id(0)
    first = (i % tiles_per_stream) == 0
    lane = lax.broadcasted_iota(jnp.int32, (1, LANE), 1)
    lo = lane < HEAD_DIM

    def slabs(x):
        xs = pltpu.roll(x, HEAD_DIM, 1)
        return (jnp.where(lo, x, xs).astype(BF16), jnp.where(lo, xs, x).astype(BF16))

    km, vm = slabs(km_ref[...]), slabs(vm_ref[...])
    kp, vp = slabs(kp_ref[...]), slabs(vp_ref[...])
    ko, vo = slabs(ko_ref[...]), slabs(vo_ref[...])
    zpad = jnp.zeros((KEY_PAD - N_KEYS, LANE), BF16)
    kpos = lax.broadcasted_iota(jnp.int32, (1, KEY_PAD), 1)
    qrow = lax.broadcasted_iota(jnp.int32, (4 * CHUNK, 1), 0)
    q_per_kv = N_HEADS // N_KV

    def window(prev, own, c):
        parts = []
        if CHUNK * c < WINDOW:
            parts.append(prev[CHUNK * c:WINDOW])
        parts.append(own[max(CHUNK * c - WINDOW, 0):CHUNK * (c + 1)])
        return parts

    for c in range(nchunk):
        qc = q_ref[c * CHUNK:(c + 1) * CHUNK, :]
        for kv in range(N_KV):
            kc = jnp.concatenate([km[kv]] + window(kp[kv], ko[kv], c) + [zpad], axis=0)
            vc = jnp.concatenate([vm[kv]] + window(vp[kv], vo[kv], c) + [zpad], axis=0)
            qs = []
            for g in range(q_per_kv):
                h = kv * q_per_kv + g
                slab = qc[:, (h // 2) * LANE:(h // 2 + 1) * LANE]
                keep = lo if h % 2 == 0 else jnp.logical_not(lo)
                qs.append(jnp.where(keep, slab, jnp.zeros_like(slab)))
            qst = jnp.concatenate(qs, axis=0)
            s = lax.dot_general(qst, kc, (((1,), (1,)), ((), ())), preferred_element_type=F32)
            bad = kpos >= N_KEYS
            n_masked = WINDOW - CHUNK * c
            if mask_first and n_masked > 0:
                lim = N_META + jnp.where(first, n_masked, 0)
                bad = bad | ((kpos >= N_META) & (kpos < lim))
            s = jnp.where(bad, -jnp.inf, s)
            sink = jnp.full((4 * CHUNK, 1), sink_ref[kv * q_per_kv + q_per_kv - 1], F32)
            for g in range(q_per_kv - 2, -1, -1):
                sink = jnp.where(qrow < (g + 1) * CHUNK, sink_ref[kv * q_per_kv + g], sink)
            m = jnp.maximum(jnp.max(s, axis=-1, keepdims=True), sink)
            p = jnp.exp(s - m)
            denom = jnp.sum(p, axis=-1, keepdims=True) + jnp.exp(sink - m)
            o = _dot(p.astype(BF16), vc) / denom
            for jj in range(q_per_kv // 2):
                g0 = 2 * jj
                oslab = jnp.where(lo, o[g0 * CHUNK:(g0 + 1) * CHUNK, :],
                                  o[(g0 + 1) * CHUNK:(g0 + 2) * CHUNK, :])
                j = kv * (q_per_kv // 2) + jj
                o_ref[c * CHUNK:(c + 1) * CHUNK, j * LANE:(j + 1) * LANE] = oslab.astype(BF16)


def _attn_call(sinks, q, kmeta, vmeta, kprev, vprev, kown, vown, *, tq, tiles_per_stream,
               mask_first, prev_map, meta_map, name):
    n = q.shape[0]
    nt = n // tq
    kern = functools.partial(_attn_kernel, nchunk=tq // CHUNK, tiles_per_stream=tiles_per_stream,
                             mask_first=mask_first)
    meta_spec = pl.BlockSpec((N_META, D_KV), meta_map)
    prev_spec = pl.BlockSpec((WINDOW, D_KV), prev_map)
    own_spec = pl.BlockSpec((tq, D_KV), lambda i: (i, 0))
    return pl.pallas_call(
        kern,
        grid=(nt,),
        in_specs=[pl.BlockSpec(memory_space=pltpu.SMEM),
                  pl.BlockSpec((tq, D_ATTN), lambda i: (i, 0)),
                  meta_spec, meta_spec, prev_spec, prev_spec, own_spec, own_spec],
        out_specs=pl.BlockSpec((tq, D_ATTN), lambda i: (i, 0)),
        out_shape=jax.ShapeDtypeStruct((n, D_ATTN), BF16),
        compiler_params=_params(),
        name=name,
    )(sinks, q, kmeta, vmeta, kprev, vprev, kown, vown)


def _merge_kernel(o_ref, rnn_ref, g_ref, x_ref, lng_ref, lnb_ref, wpa_ref, wpr_ref, wo_ref,
                  l1g_ref, l1b_ref, wr_ref, br_ref, cin_ref,
                  z1_ref, route_ref, cout_ref, cnt):
    i = pl.program_id(0)

    @pl.when(i == 0)
    def _():
        cnt[...] = cin_ref[...]

    g = g_ref[...]
    pa = _dot(o_ref[...], wpa_ref[...])
    pr = _dot(rnn_ref[...], wpr_ref[...])
    mixed = g[:, :D_MODEL].astype(F32) * pa + g[:, D_MODEL:].astype(F32) * pr
    mix = _dot(mixed.astype(BF16), wo_ref[...])
    zn = _layer_norm(x_ref[...], lng_ref[...], lnb_ref[...])
    z1 = _layer_norm(ALPHA * zn + mix, l1g_ref[...], l1b_ref[...])
    z1_ref[...] = z1

    z_hi = z1.astype(BF16)
    z_lo = (z1 - z_hi.astype(F32)).astype(BF16)
    la = _dot(z_hi, wr_ref[...])
    lb = _dot(z_lo, wr_ref[...])
    logit = la[:, :LANE] + la[:, LANE:] + lb[:, :LANE] + lb[:, LANE:] + br_ref[...]

    tm = logit.shape[0]
    lane = lax.broadcasted_iota(jnp.int32, (tm, LANE), 1).astype(F32)
    ninf = -jnp.inf
    big = float(LANE)
    gl = jnp.where(lane < N_GROUPS, logit, ninf)
    gmax = jnp.max(gl, axis=-1, keepdims=True)
    gidx = jnp.min(jnp.where(gl == gmax, lane, big), axis=-1, keepdims=True)
    gw = 1.0 / jnp.sum(jnp.exp(gl - gmax), axis=-1, keepdims=True)
    lane_grp = jnp.floor((lane - N_GROUPS) * (1.0 / EXPERTS_PER_GROUP))
    el = jnp.where((lane >= N_GROUPS) & (lane_grp == gidx), logit, ninf)
    v1 = jnp.max(el, axis=-1, keepdims=True)
    i1 = jnp.min(jnp.where(el == v1, lane, big), axis=-1, keepdims=True)
    el2 = jnp.where(lane == i1, ninf, el)
    v2 = jnp.max(el2, axis=-1, keepdims=True)
    i2 = jnp.min(jnp.where(el2 == v2, lane, big), axis=-1, keepdims=True)
    e21 = jnp.exp(v2 - v1)
    w1 = gw / (1.0 + e21)
    w2 = gw * e21 / (1.0 + e21)
    base = N_GROUPS + EXPERTS_PER_GROUP * gidx
    l1, l2 = i1 - base, i2 - base
    first_lo = l1 < l2
    ea, eb = jnp.minimum(l1, l2), jnp.maximum(l1, l2)
    wa = jnp.where(first_lo, w1, w2)
    wb = jnp.where(first_lo, w2, w1)
    pair = ea * (7.0 - ea) * 0.5 + (eb - ea - 1.0)
    cls = gidx * N_PAIR + pair

    onehot = lane == cls
    r_i = lax.broadcasted_iota(jnp.int32, (tm, tm), 0)
    c_i = lax.broadcasted_iota(jnp.int32, (tm, tm), 1)
    tri = jnp.where(c_i < r_i, 1.0, 0.0).astype(BF16)
    before = _dot(tri, jnp.where(onehot, 1.0, 0.0).astype(BF16))
    ohf = jnp.where(onehot, 1.0, 0.0)
    rank = jnp.sum(ohf * (before + cnt[0:1, :]), axis=-1, keepdims=True)
    cnt[...] = cnt[...] + jnp.sum(ohf, axis=0, keepdims=True)
    cout_ref[...] = cnt[...]

    route_ref[...] = jnp.where(lane == 0.0, cls,
                     jnp.where(lane == 1.0, wa,
                     jnp.where(lane == 2.0, wb,
                     jnp.where(lane == 3.0, rank, 0.0))))


def _merge_call(o, rnn, g, x2d, counts_in, wts, *, name):
    n = x2d.shape[0]
    nt = n // TM

    def full(a):
        return pl.BlockSpec(a.shape, lambda i: (0,) * a.ndim)

    def rows(w):
        return pl.BlockSpec((TM, w), lambda i: (i, 0))

    return pl.pallas_call(
        _merge_kernel,
        grid=(nt,),
        in_specs=[rows(D_ATTN), rows(D_RNN), rows(2 * D_MODEL), rows(D_MODEL)]
                 + [full(w) for w in wts] + [full(counts_in)],
        out_specs=[rows(D_MODEL), rows(LANE), pl.BlockSpec((SUBLANE, LANE), lambda i: (0, 0))],
        out_shape=[jax.ShapeDtypeStruct((n, D_MODEL), F32),
                   jax.ShapeDtypeStruct((n, LANE), F32),
                   jax.ShapeDtypeStruct((SUBLANE, LANE), F32)],
        scratch_shapes=[pltpu.VMEM((SUBLANE, LANE), F32)],
        compiler_params=_params(),
        name=name,
    )(o, rnn, g, x2d, *wts, counts_in)


def _dispatch_kernel(pos_ref, z1_ref, route_ref, xs_in_ref, xs_ref, rowbuf, sem, *, nt):
    del xs_in_ref
    i = pl.program_id(0)
    slot = i % 2

    def row_copy(sl, r, p):
        return pltpu.make_async_copy(rowbuf.at[sl, pl.ds(r, 1), :], xs_ref.at[pl.ds(p, 1), :],
                                     sem.at[sl])

    def drain(sl):
        def body(r, c):
            row_copy(sl, 0, 0).wait()
            return c
        lax.fori_loop(0, TM, body, 0)

    @pl.when(i >= 2)
    def _():
        drain(slot)

    rowbuf[slot, :, 0:D_MODEL] = z1_ref[...]
    rowbuf[slot, :, D_MODEL:XS_W] = route_ref[...]

    def issue(r, c):
        row_copy(slot, r, pos_ref[0, r]).start()
        return c
    lax.fori_loop(0, TM, issue, 0)

    @pl.when(i == nt - 1)
    def _():
        drain(slot)
        if nt > 1:
            drain(1 - slot)


def _dispatch_call(pos3, z1, route, xs, *, name):
    n = z1.shape[0]
    nt = n // TM
    return pl.pallas_call(
        functools.partial(_dispatch_kernel, nt=nt),
        grid=(nt,),
        in_specs=[pl.BlockSpec((None, 1, TM), lambda i: (i, 0, 0), memory_space=pltpu.SMEM),
                  pl.BlockSpec((TM, D_MODEL), lambda i: (i, 0)),
                  pl.BlockSpec((TM, LANE), lambda i: (i, 0)),
                  pl.BlockSpec(memory_space=pl.ANY)],
        out_specs=pl.BlockSpec(memory_space=pl.ANY),
        out_shape=jax.ShapeDtypeStruct(xs.shape, xs.dtype),
        input_output_aliases={3: 0},
        scratch_shapes=[pltpu.VMEM((2, TM, XS_W), F32), pltpu.SemaphoreType.DMA((2,))],
        compiler_params=_params(),
        name=name,
    )(pos3, z1, route, xs)


def _moe_kernel(grp_ref, ea_ref, eb_ref, val_ref, xs_ref, wg_ref, wu_ref, wd_ref, y_ref):
    del grp_ref
    i = pl.program_id(0)

    @pl.when(val_ref[i] == 1)
    def _():
        ea = ea_ref[i]
        eb = eb_ref[i]
        xb = xs_ref[:, 0:D_MODEL].astype(BF16)
        scal = xs_ref[:, D_MODEL:XS_W]

        def hidden(e, w):
            h = jax.nn.silu(_dot(xb, wg_ref[e])) * _dot(xb, wu_ref[e])
            return (h * w).astype(BF16)

        ha = hidden(ea, scal[:, 1:2])
        hb = hidden(eb, scal[:, 2:3])
        y_ref[...] = _dot(ha, wd_ref[ea]) + _dot(hb, wd_ref[eb])

    @pl.when(val_ref[i] == 0)
    def _():
        y_ref[...] = jnp.zeros_like(y_ref)


def _moe_call(tile_grp, tile_ea, tile_eb, tile_valid, xs, wg, wu, wd):
    n_tiles = xs.shape[0] // TS
    wspec_in = pl.BlockSpec((None, EXPERTS_PER_GROUP, D_MODEL, D_EXPERT),
                            lambda i, grp, ea, eb, val: (grp[i], 0, 0, 0))
    wspec_out = pl.BlockSpec((None, EXPERTS_PER_GROUP, D_EXPERT, D_MODEL),
                             lambda i, grp, ea, eb, val: (grp[i], 0, 0, 0))
    grid_spec = pltpu.PrefetchScalarGridSpec(
        num_scalar_prefetch=4,
        grid=(n_tiles,),
        in_specs=[pl.BlockSpec((TS, XS_W), lambda i, grp, ea, eb, val: (i, 0)),
                  wspec_in, wspec_in, wspec_out],
        out_specs=pl.BlockSpec((TS, D_MODEL), lambda i, grp, ea, eb, val: (i, 0)),
    )
    return pl.pallas_call(
        _moe_kernel,
        grid_spec=grid_spec,
        out_shape=jax.ShapeDtypeStruct((xs.shape[0], D_MODEL), F32),
        compiler_params=_params(),
        name="moe_experts",
    )(tile_grp, tile_ea, tile_eb, tile_valid, xs, wg, wu, wd)


def _combine_kernel(pos_ref, posn_ref, z1_ref, l2g_ref, l2b_ref, ys_ref, out_ref, ybuf, sem, *, nt):
    i = pl.program_id(0)
    slot = i % 2

    def row_copy(sl, r, p):
        return pltpu.make_async_copy(ys_ref.at[pl.ds(p, 1), :], ybuf.at[sl, pl.ds(r, 1), :],
                                     sem.at[sl])

    def issue(sl, pref):
        def body(r, c):
            row_copy(sl, r, pref[0, r]).start()
            return c
        lax.fori_loop(0, TM, body, 0)

    @pl.when(i == 0)
    def _():
        issue(0, pos_ref)

    @pl.when(i + 1 < nt)
    def _():
        issue(1 - slot, posn_ref)

    def drain(r, c):
        row_copy(slot, 0, 0).wait()
        return c
    lax.fori_loop(0, TM, drain, 0)

    out_ref[...] = _layer_norm(ALPHA * z1_ref[...] + ybuf[slot], l2g_ref[...], l2b_ref[...])


def _combine_call(pos3, z1, l2g, l2b, ys, *, name):
    n = z1.shape[0]
    nt = n // TM
    return pl.pallas_call(
        functools.partial(_combine_kernel, nt=nt),
        grid=(nt,),
        in_specs=[pl.BlockSpec((None, 1, TM), lambda i: (i, 0, 0), memory_space=pltpu.SMEM),
                  pl.BlockSpec((None, 1, TM), lambda i: (jnp.minimum(i + 1, nt - 1), 0, 0),
                               memory_space=pltpu.SMEM),
                  pl.BlockSpec((TM, D_MODEL), lambda i: (i, 0)),
                  pl.BlockSpec(l2g.shape, lambda i: (0, 0)),
                  pl.BlockSpec(l2b.shape, lambda i: (0, 0)),
                  pl.BlockSpec(memory_space=pl.ANY)],
        out_specs=pl.BlockSpec((TM, D_MODEL), lambda i: (i, 0)),
        out_shape=jax.ShapeDtypeStruct((n, D_MODEL), F32),
        scratch_shapes=[pltpu.VMEM((2, TM, D_MODEL), F32), pltpu.SemaphoreType.DMA((2,))],
        compiler_params=_params(),
        name=name,
    )(pos3, pos3, z1, l2g, l2b, ys)


def _rope_tables(pos):
    half = ROPE_DIM // 2
    inv_freq = ROPE_THETA ** (-jnp.arange(half, dtype=F32) / half)
    ang = pos.astype(F32)[:, None] * inv_freq[None, :]
    cos, sin = jnp.cos(ang), jnp.sin(ang)
    p = pos.shape[0]
    ones = jnp.ones((p, HEAD_DIM - ROPE_DIM), F32)
    zeros = jnp.zeros((p, HEAD_DIM - ROPE_DIM), F32)
    zh = jnp.zeros((p, half), F32)
    c = jnp.concatenate([cos, cos, ones], axis=1)
    s1 = jnp.concatenate([zh, sin, zeros], axis=1)
    s2 = jnp.concatenate([-sin, zh, zeros], axis=1)
    rep = LANE // HEAD_DIM
    return tuple(jnp.tile(t, (1, rep)) for t in (c, s1, s2))


def kernel(x_prompt, x_sample, cache_meta_k, cache_meta_v, cache_win_k, cache_win_v, state_conv, state_h, meta_tokens, ln_in_g, ln_in_b, w_in, b_gate, attn_sinks, conv_w, conv_b, w_rg_a, b_rg_a, w_rg_x, b_rg_x, lru_lambda, w_branch_attn, w_branch_rnn, w_out, ln1_g, ln1_b, w_group, b_group, w_router, b_router, w_gate, w_up, w_down, ln2_g, ln2_b):
    B, S, _ = x_prompt.shape
    DB, TSQ, _ = x_sample.shape
    assert DEPTH == 1 and w_in.shape == (DEPTH, D_MODEL, D_IN)
    assert S % TM == 0 and (DB * TSQ) % TM == 0 and TM % TSQ == 0 and TSQ == CHUNK
    assert cache_win_k.shape[2] == WINDOW and S >= WINDOW
    l = 0

    def row(v):
        return v.reshape(1, -1).astype(F32)

    lng, lnb = row(ln_in_g), row(ln_in_b)
    win = w_in[l].astype(BF16)
    cw = jnp.concatenate([conv_w[l], jnp.zeros((SUBLANE - CONV_W, D_RNN), F32)], axis=0)
    wrg = jnp.concatenate([w_rg_a[l], w_rg_x[l]], axis=-1).astype(BF16)
    in_wts = (lng, lnb, win, row(b_gate[l]), cw, row(conv_b[l]), wrg,
              row(b_rg_a[l]), row(b_rg_x[l]), row(lru_lambda[l]))
    wr = jnp.concatenate([w_group[l], w_router[l].reshape(D_MODEL, -1)], axis=1)
    wr = jnp.pad(wr, ((0, 0), (0, LANE - wr.shape[1])))
    wr_hi = wr.astype(BF16)
    wr_lo = (wr - wr_hi.astype(F32)).astype(BF16)
    wr_cat = jnp.concatenate([wr_hi, wr_lo], axis=1)
    br = jnp.pad(jnp.concatenate([b_group[l], b_router[l].reshape(-1)]),
                 (0, LANE - N_GROUPS * (1 + EXPERTS_PER_GROUP))).reshape(1, LANE)
    merge_wts = (lng, lnb, w_branch_attn[l].astype(BF16), w_branch_rnn[l].astype(BF16),
                 w_out[l].astype(BF16), row(ln1_g[l]), row(ln1_b[l]), wr_cat, br)
    wg = w_gate[l].astype(BF16).reshape(N_GROUPS, EXPERTS_PER_GROUP, D_MODEL, D_EXPERT)
    wu = w_up[l].astype(BF16).reshape(N_GROUPS, EXPERTS_PER_GROUP, D_MODEL, D_EXPERT)
    wd = w_down[l].astype(BF16).reshape(N_GROUPS, EXPERTS_PER_GROUP, D_EXPERT, D_MODEL)
    sinks = attn_sinks[l].astype(F32)

    rope_meta = _rope_tables(jnp.arange(N_META))
    rope_p = _rope_tables(N_META + jnp.arange(S))
    rope_s = _rope_tables(N_META + PAST_LEN + jnp.arange(TSQ))
    rope_s = tuple(jnp.tile(t, (TM // TSQ, 1)) for t in rope_s)

    xp = x_prompt.reshape(B * S, D_MODEL)
    xs_ = x_sample.reshape(DB * TSQ, D_MODEL)
    n_p, n_s = B * S, DB * TSQ
    zero_state = jnp.zeros((1, SUBLANE, D_RNN), F32)

    _, k_m, v_m, _, _, conv_m, h_m = _inproj_call(
        meta_tokens.astype(F32), rope_meta, zero_state, zero_state, in_wts,
        tm=N_META, seg=N_META, tiles_per_stream=1, rope_tiles=1, name="inproj_meta")
    q_p, k_p, v_p, rnn_p, g_p, conv_p, h_p = _inproj_call(
        xp, rope_p, conv_m, h_m, in_wts,
        tm=TM, seg=TM, tiles_per_stream=S // TM, rope_tiles=S // TM, name="inproj_prompt")
    iconv_s = jnp.pad(state_conv[l].astype(F32), ((0, 0), (SUBLANE - (CONV_W - 1), 0), (0, 0)))
    ih_s = jnp.broadcast_to(state_h[l].astype(F32)[:, None, :], (DB, SUBLANE, D_RNN))
    q_s, k_s, v_s, rnn_s, g_s, conv_s, h_s = _inproj_call(
        xs_, rope_s, iconv_s, ih_s, in_wts,
        tm=TM, seg=TSQ, tiles_per_stream=1, rope_tiles=1, name="inproj_sample")

    tps = S // TM
    win_blocks = TM // WINDOW
    o_p = _attn_call(
        sinks, q_p, k_m, v_m, k_p, v_p, k_p, v_p, tq=TM, tiles_per_stream=tps, mask_first=True,
        prev_map=lambda i: (jnp.maximum(i * win_blocks - 1, 0), 0),
        meta_map=lambda i: (0, 0), name="attn_prompt")
    cmk = cache_meta_k[l].reshape(DB * N_META, D_KV).astype(F32)
    cmv = cache_meta_v[l].reshape(DB * N_META, D_KV).astype(F32)
    cwk = cache_win_k[l].reshape(DB * WINDOW, D_KV).astype(F32)
    cwv = cache_win_v[l].reshape(DB * WINDOW, D_KV).astype(F32)
    o_s = _attn_call(
        sinks, q_s, cmk, cmv, cwk, cwv, k_s, v_s, tq=TSQ, tiles_per_stream=1, mask_first=False,
        prev_map=lambda i: (i, 0), meta_map=lambda i: (i, 0), name="attn_sample")

    zero_cnt = jnp.zeros((SUBLANE, LANE), F32)
    z1_p, route_p, cnt_p = _merge_call(o_p, rnn_p, g_p, xp, zero_cnt, merge_wts, name="merge_prompt")
    z1_s, route_s, cnt_all = _merge_call(o_s, rnn_s, g_s, xs_, cnt_p, merge_wts, name="merge_sample")

    counts = cnt_all[0, :N_CLASS].astype(jnp.int32)
    padded = ((counts + TS - 1) // TS) * TS
    ends = jnp.cumsum(padded)
    offsets = ends - padded
    n_tiles = (n_p + n_s) // TS + N_CLASS
    n_slots = n_tiles * TS

    def slots(route):
        cls = route[:, 0].astype(jnp.int32)
        return (offsets[cls] + route[:, 3].astype(jnp.int32)).reshape(-1, 1, TM)

    pos_p, pos_s = slots(route_p), slots(route_s)
    tile_start = jnp.arange(n_tiles, dtype=jnp.int32) * TS
    tile_valid = (tile_start < ends[-1]).astype(jnp.int32)
    last_cls = jnp.sum((ends <= ends[-1] - 1).astype(jnp.int32))
    tile_cls = jnp.minimum(jnp.sum((tile_start[:, None] >= ends[None, :]).astype(jnp.int32), axis=1),
                           jnp.minimum(last_cls, N_CLASS - 1))
    tile_grp = tile_cls // N_PAIR
    tile_ea = jnp.asarray(PAIR_A)[tile_cls % N_PAIR]
    tile_eb = jnp.asarray(PAIR_B)[tile_cls % N_PAIR]

    xs0 = jnp.zeros((n_slots, XS_W), F32)
    xs1 = _dispatch_call(pos_p, z1_p, route_p, xs0, name="dispatch_prompt")
    xs2 = _dispatch_call(pos_s, z1_s, route_s, xs1, name="dispatch_sample")
    ys = _moe_call(tile_grp, tile_ea, tile_eb, tile_valid, xs2, wg, wu, wd)
    l2g, l2b = row(ln2_g[l]), row(ln2_b[l])
    y_p = _combine_call(pos_p, z1_p, l2g, l2b, ys, name="combine_prompt")
    y_s = _combine_call(pos_s, z1_s, l2g, l2b, ys, name="combine_sample")

    def heads(a, lead):
        return a.reshape(*lead, N_KV, HEAD_DIM)[None]

    tail = slice(SUBLANE - (CONV_W - 1), SUBLANE)
    y_prompt = y_p.reshape(B, S, D_MODEL)
    y_sample = y_s.reshape(DB, TSQ, D_MODEL)
    meta_k_p = jnp.broadcast_to(heads(k_m, (1, N_META)), (1, B, N_META, N_KV, HEAD_DIM))
    meta_v_p = jnp.broadcast_to(heads(v_m, (1, N_META)), (1, B, N_META, N_KV, HEAD_DIM))
    win_k_p = heads(k_p.reshape(B, S, D_KV)[:, S - WINDOW:], (B, WINDOW))
    win_v_p = heads(v_p.reshape(B, S, D_KV)[:, S - WINDOW:], (B, WINDOW))
    conv_out_p = conv_p[:, tail, :][None]
    h_out_p = h_p[:, 0, :][None]
    new_k_s = heads(k_s, (DB, TSQ))
    new_v_s = heads(v_s, (DB, TSQ))
    conv_out_s = conv_s[:, tail, :][None]
    h_out_s = h_s[:, 0, :][None].astype(state_h.dtype)
    return (y_prompt, y_sample, meta_k_p, meta_v_p, win_k_p, win_v_p, conv_out_p, h_out_p,
            new_k_s, new_v_s, conv_out_s, h_out_s)
```

```python
import functools

import numpy as np
import jax
import jax.numpy as jnp
from jax import lax
from jax.experimental import pallas as pl
from jax.experimental.pallas import tpu as pltpu

F32 = jnp.float32
BF16 = jnp.bfloat16

D_MODEL = 1024
N_META = 16
CHUNK = 64
N_HEADS = 8
N_KV = 2
HEAD_DIM = 64
D_ATTN = N_HEADS * HEAD_DIM
D_KV = N_KV * HEAD_DIM
WINDOW = 128
PAST_LEN = 4096
ROPE_DIM = HEAD_DIM // 4
ROPE_THETA = 500000.0
D_RNN = D_MODEL
N_RNN_BLOCKS = 8
RNN_BLOCK = D_RNN // N_RNN_BLOCKS
CONV_W = 4
LRU_C = 8.0
N_GROUPS = 4
EXPERTS_PER_GROUP = 4
D_EXPERT = 512
DEPTH = 1
ALPHA = (2 * DEPTH) ** 0.25
LN_EPS = 1e-5

C_Q, C_K, C_V = 0, D_ATTN, D_ATTN + D_KV
C_XR = D_ATTN + 2 * D_KV
C_GR = C_XR + D_RNN
C_GL = C_GR + D_RNN
D_IN = C_GL + 2 * D_MODEL

LANE = 128
SUBLANE = 8
TM = 256
TS = 256
DMA_UNROLL = 8
N_PAIR = 6
N_CLASS = N_GROUPS * N_PAIR
XS_W = D_MODEL + LANE
VMEM_LIMIT = 56 * 1024 * 1024

PAIR_A = np.array([0, 0, 0, 1, 1, 2], np.int32)
PAIR_B = np.array([1, 2, 3, 2, 3, 3], np.int32)


def _params(n_grid_dims=1):
    return pltpu.CompilerParams(
        dimension_semantics=("arbitrary",) * n_grid_dims, vmem_limit_bytes=VMEM_LIMIT)


def _layer_norm(x, g, b):
    mu = jnp.mean(x, axis=-1, keepdims=True)
    xc = x - mu
    var = jnp.mean(xc * xc, axis=-1, keepdims=True)
    return xc * lax.rsqrt(var + LN_EPS) * g + b


def _dot(a, b):
    return jnp.dot(a, b, preferred_element_type=F32)


def _inproj_kernel(x_ref, lng_ref, lnb_ref, win_ref, bgate_ref, rc_ref, rs1_ref, rs2_ref,
                   cw_ref, cb_ref, wrg_ref, ba_ref, bx_ref, lam_ref, iconv_ref, ih_ref,
                   q_ref, k_ref, v_ref, rnn_ref, g_ref, conv_ref, h_ref,
                   xbuf, abuf, ubuf, hcar, *, seg, nseg, tiles_per_stream):
    i = pl.program_id(0)
    zn = _layer_norm(x_ref[...], lng_ref[...], lnb_ref[...]).astype(BF16)

    qkv = _dot(zn, win_ref[:, C_Q:C_XR])
    rc, rs1, rs2 = rc_ref[...], rs1_ref[...], rs2_ref[...]

    def rope(s):
        return s * rc + pltpu.roll(s, 8, 1) * rs1 + pltpu.roll(s, LANE - 8, 1) * rs2

    for j in range(D_ATTN // LANE):
        q_ref[:, j * LANE:(j + 1) * LANE] = (
            rope(qkv[:, j * LANE:(j + 1) * LANE]) * (HEAD_DIM ** -0.5)).astype(BF16)
    k_ref[...] = rope(qkv[:, C_K:C_V])
    v_ref[...] = qkv[:, C_V:C_XR]

    gl = _dot(zn, win_ref[:, C_GL:D_IN]) + bgate_ref[...]
    g_ref[...] = jax.nn.sigmoid(gl).astype(BF16)

    xr = _dot(zn, win_ref[:, C_XR:C_GR])
    gr = _dot(zn, win_ref[:, C_GR:C_GL])

    c8 = LRU_C * jax.nn.softplus(-lam_ref[...])
    row = lax.broadcasted_iota(jnp.int32, (SUBLANE, D_RNN), 0)
    cw = cw_ref[...]

    for s in range(nseg):
        r0 = s * seg

        def load_init():
            xbuf[0:SUBLANE, :] = iconv_ref[s]
            hcar[...] = ih_ref[s]

        if tiles_per_stream == 1:
            load_init()
        else:
            pl.when(i % tiles_per_stream == 0)(load_init)

        xbuf[SUBLANE:SUBLANE + seg, :] = xr[r0:r0 + seg, :]
        xc = (xbuf[SUBLANE:SUBLANE + seg, :] * cw[3:4, :]
              + xbuf[SUBLANE - 1:SUBLANE - 1 + seg, :] * cw[2:3, :]
              + xbuf[SUBLANE - 2:SUBLANE - 2 + seg, :] * cw[1:2, :]
              + xbuf[SUBLANE - 3:SUBLANE - 3 + seg, :] * cw[0:1, :]
              + cb_ref[...])
        tail = xbuf[seg:seg + SUBLANE, :]
        conv_ref[s] = tail
        xbuf[0:SUBLANE, :] = tail

        for n in range(N_RNN_BLOCKS):
            sl = slice(n * RNN_BLOCK, (n + 1) * RNN_BLOCK)
            xcn = xc[:, sl]
            ri = _dot(xcn.astype(BF16), wrg_ref[n])
            r = jax.nn.sigmoid(ri[:, :RNN_BLOCK] + ba_ref[:, sl])
            ig = jax.nn.sigmoid(ri[:, RNN_BLOCK:] + bx_ref[:, sl])
            log_a = -c8[:, sl] * r
            a = jnp.exp(log_a)
            mult = jnp.sqrt(-jnp.tanh(log_a) * (a * a + 1.0))
            abuf[0:seg, sl] = a
            ubuf[0:seg, sl] = mult * ig * xcn

        def group(gi, h):
            o = pl.multiple_of(gi * SUBLANE, SUBLANE)
            a = abuf[pl.ds(o, SUBLANE), :]
            u = ubuf[pl.ds(o, SUBLANE), :]
            for sh in (1, 2, 4):
                keep = row >= sh
                u = u + a * jnp.where(keep, pltpu.roll(u, sh, 0), 0.0)
                a = a * jnp.where(keep, pltpu.roll(a, sh, 0), 1.0)
            hg = a * h + u
            ubuf[pl.ds(o, SUBLANE), :] = hg
            return hg[SUBLANE - 1:SUBLANE, :]

        h_last = lax.fori_loop(0, seg // SUBLANE, group, hcar[0:1, :])
        h_b = jnp.broadcast_to(h_last, (SUBLANE, D_RNN))
        hcar[...] = h_b
        h_ref[s] = h_b
        rnn_ref[r0:r0 + seg, :] = (ubuf[0:seg, :] * jax.nn.gelu(gr[r0:r0 + seg, :])).astype(BF16)


def _inproj_call(x2d, rope_tabs, iconv, ih, wts, *, tm, seg, tiles_per_stream, rope_tiles, name):
    n = x2d.shape[0]
    nt = n // tm
    nseg = tm // seg
    n_streams = n // (seg * tiles_per_stream) if tiles_per_stream > 1 else n // seg
    shared_init = iconv.shape[0] == 1

    def full(a):
        return pl.BlockSpec(a.shape, lambda i: (0,) * a.ndim)

    def rows(w):
        return pl.BlockSpec((tm, w), lambda i: (i, 0))

    rope_spec = pl.BlockSpec((tm, LANE), lambda i: (i % rope_tiles, 0))
    if shared_init:
        init_spec = pl.BlockSpec((1, SUBLANE, D_RNN), lambda i: (0, 0, 0))
    else:
        init_spec = pl.BlockSpec((nseg, SUBLANE, D_RNN), lambda i: (i, 0, 0))
    state_spec = pl.BlockSpec((nseg, SUBLANE, D_RNN), lambda i: (i // tiles_per_stream, 0, 0))

    (lng, lnb, win, bgate, cw, cb, wrg, ba, bx, lam) = wts
    kern = functools.partial(_inproj_kernel, seg=seg, nseg=nseg, tiles_per_stream=tiles_per_stream)
    return pl.pallas_call(
        kern,
        grid=(nt,),
        in_specs=[rows(D_MODEL), full(lng), full(lnb), full(win), full(bgate),
                  rope_spec, rope_spec, rope_spec,
                  full(cw), full(cb), full(wrg), full(ba), full(bx), full(lam),
                  init_spec, init_spec],
        out_specs=[rows(D_ATTN), rows(D_KV), rows(D_KV), rows(D_RNN), rows(2 * D_MODEL),
                   state_spec, state_spec],
        out_shape=[jax.ShapeDtypeStruct((n, D_ATTN), BF16),
                   jax.ShapeDtypeStruct((n, D_KV), F32),
                   jax.ShapeDtypeStruct((n, D_KV), F32),
                   jax.ShapeDtypeStruct((n, D_RNN), BF16),
                   jax.ShapeDtypeStruct((n, 2 * D_MODEL), BF16),
                   jax.ShapeDtypeStruct((n_streams, SUBLANE, D_RNN), F32),
                   jax.ShapeDtypeStruct((n_streams, SUBLANE, D_RNN), F32)],
        scratch_shapes=[pltpu.VMEM((SUBLANE + seg, D_RNN), F32),
                        pltpu.VMEM((seg, D_RNN), F32),
                        pltpu.VMEM((seg, D_RNN), F32),
                        pltpu.VMEM((SUBLANE, D_RNN), F32)],
        compiler_params=_params(),
        name=name,
    )(x2d, lng, lnb, win, bgate, *rope_tabs, cw, cb, wrg, ba, bx, lam, iconv, ih)


N_KEYS = N_META + WINDOW + CHUNK
KEY_PAD = 2 * LANE


def _attn_kernel(sink_ref, q_ref, km_ref, vm_ref, kp_ref, vp_ref, ko_ref, vo_ref, o_ref,
                 *, nchunk, tiles_per_stream, mask_first):
    i = pl.program_id(0)
    first = (i % tiles_per_stream) == 0
    lane = lax.broadcasted_iota(jnp.int32, (1, LANE), 1)
    lo = lane < HEAD_DIM

    def slabs(x):
        xs = pltpu.roll(x, HEAD_DIM, 1)
        return (jnp.where(lo, x, xs).astype(BF16), jnp.where(lo, xs, x).astype(BF16))

    km, vm = slabs(km_ref[...]), slabs(vm_ref[...])
    kp, vp = slabs(kp_ref[...]), slabs(vp_ref[...])
    ko, vo = slabs(ko_ref[...]), slabs(vo_ref[...])
    zpad = jnp.zeros((KEY_PAD - N_KEYS, LANE), BF16)
    kpos = lax.broadcasted_iota(jnp.int32, (1, KEY_PAD), 1)
    qrow = lax.broadcasted_iota(jnp.int32, (4 * CHUNK, 1), 0)
    q_per_kv = N_HEADS // N_KV

    def window(prev, own, c):
        parts = []
        if CHUNK * c < WINDOW:
            parts.append(prev[CHUNK * c:WINDOW])
        parts.append(own[max(CHUNK * c - WINDOW, 0):CHUNK * (c + 1)])
        return parts

    for c in range(nchunk):
        qc = q_ref[c * CHUNK:(c + 1) * CHUNK, :]
        for kv in range(N_KV):
            kc = jnp.concatenate([km[kv]] + window(kp[kv], ko[kv], c) + [zpad], axis=0)
            vc = jnp.concatenate([vm[kv]] + window(vp[kv], vo[kv], c) + [zpad], axis=0)
            qs = []
            for g in range(q_per_kv):
                h = kv * q_per_kv + g
                slab = qc[:, (h // 2) * LANE:(h // 2 + 1) * LANE]
                keep = lo if h % 2 == 0 else jnp.logical_not(lo)
                qs.append(jnp.where(keep, slab, jnp.zeros_like(slab)))
            qst = jnp.concatenate(qs, axis=0)
            s = lax.dot_general(qst, kc, (((1,), (1,)), ((), ())), preferred_element_type=F32)
            bad = kpos >= N_KEYS
            n_masked = WINDOW - CHUNK * c
            if mask_first and n_masked > 0:
                lim = N_META + jnp.where(first, n_masked, 0)
                bad = bad | ((kpos >= N_META) & (kpos < lim))
            s = jnp.where(bad, -jnp.inf, s)
            sink = jnp.full((4 * CHUNK, 1), sink_ref[kv * q_per_kv + q_per_kv - 1], F32)
            for g in range(q_per_kv - 2, -1, -1):
                sink = jnp.where(qrow < (g + 1) * CHUNK, sink_ref[kv * q_per_kv + g], sink)
            m = jnp.maximum(jnp.max(s, axis=-1, keepdims=True), sink)
            p = jnp.exp(s - m)
            denom = jnp.sum(p, axis=-1, keepdims=True) + jnp.exp(sink - m)
            o = _dot(p.astype(BF16), vc) / denom
            for jj in range(q_per_kv // 2):
                g0 = 2 * jj
                oslab = jnp.where(lo, o[g0 * CHUNK:(g0 + 1) * CHUNK, :],
                                  o[(g0 + 1) * CHUNK:(g0 + 2) * CHUNK, :])
                j = kv * (q_per_kv // 2) + jj
                o_ref[c * CHUNK:(c + 1) * CHUNK, j * LANE:(j + 1) * LANE] = oslab.astype(BF16)


def _attn_call(sinks, q, kmeta, vmeta, kprev, vprev, kown, vown, *, tq, tiles_per_stream,
               mask_first, prev_map, meta_map, name):
    n = q.shape[0]
    nt = n // tq
    kern = functools.partial(_attn_kernel, nchunk=tq // CHUNK, tiles_per_stream=tiles_per_stream,
                             mask_first=mask_first)
    meta_spec = pl.BlockSpec((N_META, D_KV), meta_map)
    prev_spec = pl.BlockSpec((WINDOW, D_KV), prev_map)
    own_spec = pl.BlockSpec((tq, D_KV), lambda i: (i, 0))
    return pl.pallas_call(
        kern,
        grid=(nt,),
        in_specs=[pl.BlockSpec(memory_space=pltpu.SMEM),
                  pl.BlockSpec((tq, D_ATTN), lambda i: (i, 0)),
                  meta_spec, meta_spec, prev_spec, prev_spec, own_spec, own_spec],
        out_specs=pl.BlockSpec((tq, D_ATTN), lambda i: (i, 0)),
        out_shape=jax.ShapeDtypeStruct((n, D_ATTN), BF16),
        compiler_params=_params(),
        name=name,
    )(sinks, q, kmeta, vmeta, kprev, vprev, kown, vown)


def _merge_kernel(o_ref, rnn_ref, g_ref, x_ref, lng_ref, lnb_ref, wpa_ref, wpr_ref, wo_ref,
                  l1g_ref, l1b_ref, wr_ref, br_ref, cin_ref,
                  z1_ref, route_ref, slot_ref, cout_ref, cnt):
    i = pl.program_id(0)

    @pl.when(i == 0)
    def _():
        cnt[...] = cin_ref[...]

    g = g_ref[...]
    pa = _dot(o_ref[...], wpa_ref[...])
    pr = _dot(rnn_ref[...], wpr_ref[...])
    mixed = g[:, :D_MODEL].astype(F32) * pa + g[:, D_MODEL:].astype(F32) * pr
    mix = _dot(mixed.astype(BF16), wo_ref[...])
    zn = _layer_norm(x_ref[...], lng_ref[...], lnb_ref[...])
    z1 = _layer_norm(ALPHA * zn + mix, l1g_ref[...], l1b_ref[...])
    z1_ref[...] = z1

    z_hi = z1.astype(BF16)
    z_lo = (z1 - z_hi.astype(F32)).astype(BF16)
    la = _dot(z_hi, wr_ref[...])
    lb = _dot(z_lo, wr_ref[...])
    logit = la[:, :LANE] + la[:, LANE:] + lb[:, :LANE] + lb[:, LANE:] + br_ref[...]

    tm = logit.shape[0]
    lane = lax.broadcasted_iota(jnp.int32, (tm, LANE), 1).astype(F32)
    ninf = -jnp.inf
    big = float(LANE)
    gl = jnp.where(lane < N_GROUPS, logit, ninf)
    gmax = jnp.max(gl, axis=-1, keepdims=True)
    gidx = jnp.min(jnp.where(gl == gmax, lane, big), axis=-1, keepdims=True)
    gw = 1.0 / jnp.sum(jnp.exp(gl - gmax), axis=-1, keepdims=True)
    lane_grp = jnp.floor((lane - N_GROUPS) * (1.0 / EXPERTS_PER_GROUP))
    el = jnp.where((lane >= N_GROUPS) & (lane_grp == gidx), logit, ninf)
    v1 = jnp.max(el, axis=-1, keepdims=True)
    i1 = jnp.min(jnp.where(el == v1, lane, big), axis=-1, keepdims=True)
    el2 = jnp.where(lane == i1, ninf, el)
    v2 = jnp.max(el2, axis=-1, keepdims=True)
    i2 = jnp.min(jnp.where(el2 == v2, lane, big), axis=-1, keepdims=True)
    e21 = jnp.exp(v2 - v1)
    w1 = gw / (1.0 + e21)
    w2 = gw * e21 / (1.0 + e21)
    base = N_GROUPS + EXPERTS_PER_GROUP * gidx
    l1, l2 = i1 - base, i2 - base
    first_lo = l1 < l2
    ea, eb = jnp.minimum(l1, l2), jnp.maximum(l1, l2)
    wa = jnp.where(first_lo, w1, w2)
    wb = jnp.where(first_lo, w2, w1)
    pair = ea * (7.0 - ea) * 0.5 + (eb - ea - 1.0)
    cls = gidx * N_PAIR + pair

    onehot = lane == cls
    r_i = lax.broadcasted_iota(jnp.int32, (tm, tm), 0)
    c_i = lax.broadcasted_iota(jnp.int32, (tm, tm), 1)
    tri = jnp.where(c_i < r_i, 1.0, 0.0).astype(BF16)
    before = _dot(tri, jnp.where(onehot, 1.0, 0.0).astype(BF16))
    ohf = jnp.where(onehot, 1.0, 0.0)
    rank = jnp.sum(ohf * (before + cnt[0:1, :]), axis=-1, keepdims=True)
    cnt[...] = cnt[...] + jnp.sum(ohf, axis=0, keepdims=True)
    cout_ref[...] = cnt[...]

    eye = r_i == c_i
    cls_row = jnp.sum(jnp.where(eye, cls, 0.0), axis=0, keepdims=True)
    rank_row = jnp.sum(jnp.where(eye, rank, 0.0), axis=0, keepdims=True)
    sub = lax.broadcasted_iota(jnp.int32, (SUBLANE, tm), 0)
    slot_ref[...] = jnp.where(sub == 0, cls_row, jnp.where(sub == 1, rank_row, 0.0))

    route_ref[...] = jnp.where(lane == 0.0, cls,
                     jnp.where(lane == 1.0, wa,
                     jnp.where(lane == 2.0, wb,
                     jnp.where(lane == 3.0, rank, 0.0))))


def _merge_call(o, rnn, g, x2d, counts_in, wts, *, name):
    n = x2d.shape[0]
    nt = n // TM

    def full(a):
        return pl.BlockSpec(a.shape, lambda i: (0,) * a.ndim)

    def rows(w):
        return pl.BlockSpec((TM, w), lambda i: (i, 0))

    return pl.pallas_call(
        _merge_kernel,
        grid=(nt,),
        in_specs=[rows(D_ATTN), rows(D_RNN), rows(2 * D_MODEL), rows(D_MODEL)]
                 + [full(w) for w in wts] + [full(counts_in)],
        out_specs=[rows(D_MODEL), rows(LANE),
                   pl.BlockSpec((None, SUBLANE, TM), lambda i: (i, 0, 0)),
                   pl.BlockSpec((SUBLANE, LANE), lambda i: (0, 0))],
        out_shape=[jax.ShapeDtypeStruct((n, D_MODEL), F32),
                   jax.ShapeDtypeStruct((n, LANE), F32),
                   jax.ShapeDtypeStruct((nt, SUBLANE, TM), F32),
                   jax.ShapeDtypeStruct((SUBLANE, LANE), F32)],
        scratch_shapes=[pltpu.VMEM((SUBLANE, LANE), F32)],
        compiler_params=_params(),
        name=name,
    )(o, rnn, g, x2d, *wts, counts_in)


def _dispatch_kernel(pos_ref, z1_ref, route_ref, extra_ref, xs_ref, rowbuf, sem, *, nt, fresh):
    i = pl.program_id(0)
    slot = i % 2

    def row_copy(sl, r, p):
        return pltpu.make_async_copy(rowbuf.at[sl, pl.ds(r, 1), :], xs_ref.at[pl.ds(p, 1), :],
                                     sem.at[sl])

    def tile_copy(sl, t):
        return pltpu.make_async_copy(rowbuf.at[sl], xs_ref.at[pl.ds(t * TS, TS), :], sem.at[sl])

    def drain(sl):
        tile_copy(sl, 0).wait()

    if fresh:
        @pl.when(i == 0)
        def _():
            rowbuf[1] = jnp.zeros((TM, XS_W), F32)
            n_tiles = extra_ref.shape[0]

            def fill(t, c):
                @pl.when(extra_ref[t] < TS)
                def _():
                    tile_copy(1, t).start()
                return c
            lax.fori_loop(0, n_tiles, fill, 0)

            def fill_wait(t, c):
                @pl.when(extra_ref[t] < TS)
                def _():
                    tile_copy(1, 0).wait()
                return c
            lax.fori_loop(0, n_tiles, fill_wait, 0)

    @pl.when(i >= 2)
    def _():
        drain(slot)

    rowbuf[slot, :, 0:D_MODEL] = z1_ref[...]
    rowbuf[slot, :, D_MODEL:XS_W] = route_ref[...]

    def issue(rb, c):
        for u in range(DMA_UNROLL):
            r = rb * DMA_UNROLL + u
            row_copy(slot, r, pos_ref[0, r]).start(priority=u % 2)
        return c
    lax.fori_loop(0, TM // DMA_UNROLL, issue, 0)

    @pl.when(i == nt - 1)
    def _():
        drain(slot)
        if nt > 1:
            drain(1 - slot)


def _dispatch_call(pos3, z1, route, xs, tile_rows, *, name):
    n = z1.shape[0]
    nt = n // TM
    fresh = xs is None
    n_slots = tile_rows.shape[0] * TS if fresh else xs.shape[0]
    extra = tile_rows if fresh else xs
    extra_spec = pl.BlockSpec(memory_space=pltpu.SMEM if fresh else pl.ANY)
    return pl.pallas_call(
        functools.partial(_dispatch_kernel, nt=nt, fresh=fresh),
        grid=(nt,),
        in_specs=[pl.BlockSpec((None, 1, TM), lambda i: (i, 0, 0), memory_space=pltpu.SMEM),
                  pl.BlockSpec((TM, D_MODEL), lambda i: (i, 0)),
                  pl.BlockSpec((TM, LANE), lambda i: (i, 0)),
                  extra_spec],
        out_specs=pl.BlockSpec(memory_space=pl.ANY),
        out_shape=jax.ShapeDtypeStruct((n_slots, XS_W), F32),
        input_output_aliases={} if fresh else {3: 0},
        scratch_shapes=[pltpu.VMEM((2, TM, XS_W), F32), pltpu.SemaphoreType.DMA((2,))],
        compiler_params=_params(),
        name=name,
    )(pos3, z1, route, extra)


def _moe_kernel(grp_ref, ea_ref, eb_ref, val_ref, xs_ref, wg_ref, wu_ref, wd_ref, y_ref):
    del grp_ref
    i = pl.program_id(0)

    @pl.when(val_ref[i] > 0)
    def _():
        ea = ea_ref[i]
        eb = eb_ref[i]
        xb = xs_ref[:, 0:D_MODEL].astype(BF16)
        scal = xs_ref[:, D_MODEL:XS_W]

        def hidden(e, w):
            h = jax.nn.silu(_dot(xb, wg_ref[e])) * _dot(xb, wu_ref[e])
            return (h * w).astype(BF16)

        ha = hidden(ea, scal[:, 1:2])
        hb = hidden(eb, scal[:, 2:3])
        y_ref[...] = _dot(ha, wd_ref[ea]) + _dot(hb, wd_ref[eb])

    @pl.when(val_ref[i] == 0)
    def _():
        y_ref[...] = jnp.zeros_like(y_ref)


def _moe_call(tile_grp, tile_ea, tile_eb, tile_valid, xs, wg, wu, wd):
    n_tiles = xs.shape[0] // TS
    wspec_in = pl.BlockSpec((None, EXPERTS_PER_GROUP, D_MODEL, D_EXPERT),
                            lambda i, grp, ea, eb, val: (grp[i], 0, 0, 0))
    wspec_out = pl.BlockSpec((None, EXPERTS_PER_GROUP, D_EXPERT, D_MODEL),
                             lambda i, grp, ea, eb, val: (grp[i], 0, 0, 0))
    grid_spec = pltpu.PrefetchScalarGridSpec(
        num_scalar_prefetch=4,
        grid=(n_tiles,),
        in_specs=[pl.BlockSpec((TS, XS_W), lambda i, grp, ea, eb, val: (i, 0)),
                  wspec_in, wspec_in, wspec_out],
        out_specs=pl.BlockSpec((TS, D_MODEL), lambda i, grp, ea, eb, val: (i, 0)),
    )
    return pl.pallas_call(
        _moe_kernel,
        grid_spec=grid_spec,
        out_shape=jax.ShapeDtypeStruct((xs.shape[0], D_MODEL), F32),
        compiler_params=_params(),
        name="moe_experts",
    )(tile_grp, tile_ea, tile_eb, tile_valid, xs, wg, wu, wd)


def _combine_kernel(pos_ref, posn_ref, z1_ref, l2g_ref, l2b_ref, ys_ref, out_ref, ybuf, sem, *, nt):
    i = pl.program_id(0)
    slot = i % 2

    def row_copy(sl, r, p):
        return pltpu.make_async_copy(ys_ref.at[pl.ds(p, 1), :], ybuf.at[sl, pl.ds(r, 1), :],
                                     sem.at[sl])

    def issue(sl, pref):
        def body(rb, c):
            for u in range(DMA_UNROLL):
                r = rb * DMA_UNROLL + u
                row_copy(sl, r, pref[0, r]).start(priority=u % 2)
            return c
        lax.fori_loop(0, TM // DMA_UNROLL, body, 0)

    @pl.when(i == 0)
    def _():
        issue(0, pos_ref)

    @pl.when(i + 1 < nt)
    def _():
        issue(1 - slot, posn_ref)

    pltpu.make_async_copy(ys_ref.at[pl.ds(0, TM), :], ybuf.at[slot], sem.at[slot]).wait()

    out_ref[...] = _layer_norm(ALPHA * z1_ref[...] + ybuf[slot], l2g_ref[...], l2b_ref[...])


def _combine_call(pos3, z1, l2g, l2b, ys, *, name):
    n = z1.shape[0]
    nt = n // TM
    return pl.pallas_call(
        functools.partial(_combine_kernel, nt=nt),
        grid=(nt,),
        in_specs=[pl.BlockSpec((None, 1, TM), lambda i: (i, 0, 0), memory_space=pltpu.SMEM),
                  pl.BlockSpec((None, 1, TM), lambda i: (jnp.minimum(i + 1, nt - 1), 0, 0),
                               memory_space=pltpu.SMEM),
                  pl.BlockSpec((TM, D_MODEL), lambda i: (i, 0)),
                  pl.BlockSpec(l2g.shape, lambda i: (0, 0)),
                  pl.BlockSpec(l2b.shape, lambda i: (0, 0)),
                  pl.BlockSpec(memory_space=pl.ANY)],
        out_specs=pl.BlockSpec((TM, D_MODEL), lambda i: (i, 0)),
        out_shape=jax.ShapeDtypeStruct((n, D_MODEL), F32),
        scratch_shapes=[pltpu.VMEM((2, TM, D_MODEL), F32), pltpu.SemaphoreType.DMA((2,))],
        compiler_params=_params(),
        name=name,
    )(pos3, pos3, z1, l2g, l2b, ys)


def _rope_tables(pos, reps=1):
    half = ROPE_DIM // 2
    inv_freq = ROPE_THETA ** (-np.arange(half, dtype=np.float64) / half)
    ang = pos.astype(np.float64)[:, None] * inv_freq[None, :]
    cos, sin = np.cos(ang), np.sin(ang)
    p = pos.shape[0]
    ones = np.ones((p, HEAD_DIM - ROPE_DIM))
    zeros = np.zeros((p, HEAD_DIM - ROPE_DIM))
    zh = np.zeros((p, half))
    c = np.concatenate([cos, cos, ones], axis=1)
    s1 = np.concatenate([zh, sin, zeros], axis=1)
    s2 = np.concatenate([-sin, zh, zeros], axis=1)
    rep = LANE // HEAD_DIM
    return tuple(jnp.asarray(np.tile(t, (reps, rep)).astype(np.float32)) for t in (c, s1, s2))


def kernel(x_prompt, x_sample, cache_meta_k, cache_meta_v, cache_win_k, cache_win_v, state_conv, state_h, meta_tokens, ln_in_g, ln_in_b, w_in, b_gate, attn_sinks, conv_w, conv_b, w_rg_a, b_rg_a, w_rg_x, b_rg_x, lru_lambda, w_branch_attn, w_branch_rnn, w_out, ln1_g, ln1_b, w_group, b_group, w_router, b_router, w_gate, w_up, w_down, ln2_g, ln2_b):
    B, S, _ = x_prompt.shape
    DB, TSQ, _ = x_sample.shape
    assert DEPTH == 1 and w_in.shape == (DEPTH, D_MODEL, D_IN)
    assert S % TM == 0 and (DB * TSQ) % TM == 0 and TM % TSQ == 0 and TSQ == CHUNK
    assert cache_win_k.shape[2] == WINDOW and S >= WINDOW
    l = 0

    def row(v):
        return v.reshape(1, -1).astype(F32)

    lng, lnb = row(ln_in_g), row(ln_in_b)
    win = w_in[l].astype(BF16)
    cw = jnp.concatenate([conv_w[l], jnp.zeros((SUBLANE - CONV_W, D_RNN), F32)], axis=0)
    wrg = jnp.concatenate([w_rg_a[l], w_rg_x[l]], axis=-1).astype(BF16)
    in_wts = (lng, lnb, win, row(b_gate[l]), cw, row(conv_b[l]), wrg,
              row(b_rg_a[l]), row(b_rg_x[l]), row(lru_lambda[l]))
    wr = jnp.concatenate([w_group[l], w_router[l].reshape(D_MODEL, -1)], axis=1)
    wr = jnp.pad(wr, ((0, 0), (0, LANE - wr.shape[1])))
    wr_hi = wr.astype(BF16)
    wr_lo = (wr - wr_hi.astype(F32)).astype(BF16)
    wr_cat = jnp.concatenate([wr_hi, wr_lo], axis=1)
    br = jnp.pad(jnp.concatenate([b_group[l], b_router[l].reshape(-1)]),
                 (0, LANE - N_GROUPS * (1 + EXPERTS_PER_GROUP))).reshape(1, LANE)
    merge_wts = (lng, lnb, w_branch_attn[l].astype(BF16), w_branch_rnn[l].astype(BF16),
                 w_out[l].astype(BF16), row(ln1_g[l]), row(ln1_b[l]), wr_cat, br)
    wg = w_gate[l].astype(BF16).reshape(N_GROUPS, EXPERTS_PER_GROUP, D_MODEL, D_EXPERT)
    wu = w_up[l].astype(BF16).reshape(N_GROUPS, EXPERTS_PER_GROUP, D_MODEL, D_EXPERT)
    wd = w_down[l].astype(BF16).reshape(N_GROUPS, EXPERTS_PER_GROUP, D_EXPERT, D_MODEL)
    sinks = attn_sinks[l].astype(F32)

    rope_meta = _rope_tables(np.arange(N_META))
    rope_p = _rope_tables(N_META + np.arange(S))
    rope_s = _rope_tables(N_META + PAST_LEN + np.arange(TSQ), reps=TM // TSQ)

    xp = x_prompt.reshape(B * S, D_MODEL)
    xs_ = x_sample.reshape(DB * TSQ, D_MODEL)
    n_p, n_s = B * S, DB * TSQ
    zero_state = jnp.zeros((1, SUBLANE, D_RNN), F32)

    _, k_m, v_m, _, _, conv_m, h_m = _inproj_call(
        meta_tokens.astype(F32), rope_meta, zero_state, zero_state, in_wts,
        tm=N_META, seg=N_META, tiles_per_stream=1, rope_tiles=1, name="inproj_meta")
    q_p, k_p, v_p, rnn_p, g_p, conv_p, h_p = _inproj_call(
        xp, rope_p, conv_m, h_m, in_wts,
        tm=TM, seg=TM, tiles_per_stream=S // TM, rope_tiles=S // TM, name="inproj_prompt")
    iconv_s = jnp.pad(state_conv[l].astype(F32), ((0, 0), (SUBLANE - (CONV_W - 1), 0), (0, 0)))
    ih_s = jnp.broadcast_to(state_h[l].astype(F32)[:, None, :], (DB, SUBLANE, D_RNN))
    q_s, k_s, v_s, rnn_s, g_s, conv_s, h_s = _inproj_call(
        xs_, rope_s, iconv_s, ih_s, in_wts,
        tm=TM, seg=TSQ, tiles_per_stream=1, rope_tiles=1, name="inproj_sample")

    tps = S // TM
    win_blocks = TM // WINDOW
    o_p = _attn_call(
        sinks, q_p, k_m, v_m, k_p, v_p, k_p, v_p, tq=TM, tiles_per_stream=tps, mask_first=True,
        prev_map=lambda i: (jnp.maximum(i * win_blocks - 1, 0), 0),
        meta_map=lambda i: (0, 0), name="attn_prompt")
    cmk = cache_meta_k[l].reshape(DB * N_META, D_KV).astype(F32)
    cmv = cache_meta_v[l].reshape(DB * N_META, D_KV).astype(F32)
    cwk = cache_win_k[l].reshape(DB * WINDOW, D_KV).astype(F32)
    cwv = cache_win_v[l].reshape(DB * WINDOW, D_KV).astype(F32)
    o_s = _attn_call(
        sinks, q_s, cmk, cmv, cwk, cwv, k_s, v_s, tq=TSQ, tiles_per_stream=1, mask_first=False,
        prev_map=lambda i: (i, 0), meta_map=lambda i: (i, 0), name="attn_sample")

    zero_cnt = jnp.zeros((SUBLANE, LANE), F32)
    z1_p, route_p, slot_p, cnt_p = _merge_call(
        o_p, rnn_p, g_p, xp, zero_cnt, merge_wts, name="merge_prompt")
    z1_s, route_s, slot_s, cnt_all = _merge_call(
        o_s, rnn_s, g_s, xs_, cnt_p, merge_wts, name="merge_sample")

    counts = cnt_all[0, :N_CLASS].astype(jnp.int32)
    padded = ((counts + TS - 1) // TS) * TS
    ends = jnp.cumsum(padded)
    offsets = ends - padded
    n_tiles = (n_p + n_s) // TS + N_CLASS

    def slots(slot_rows):
        cls = slot_rows[:, 0:1, :].astype(jnp.int32)
        return offsets[cls] + slot_rows[:, 1:2, :].astype(jnp.int32)

    pos_p, pos_s = slots(slot_p), slots(slot_s)
    tile_start = jnp.arange(n_tiles, dtype=jnp.int32) * TS
    last_cls = jnp.sum((ends <= ends[-1] - 1).astype(jnp.int32))
    tile_cls = jnp.minimum(jnp.sum((tile_start[:, None] >= ends[None, :]).astype(jnp.int32), axis=1),
                           jnp.minimum(last_cls, N_CLASS - 1))
    tile_rows = jnp.clip((offsets + counts)[tile_cls] - tile_start, 0, TS)
    tile_grp = tile_cls // N_PAIR
    tile_ea = jnp.asarray(PAIR_A)[tile_cls % N_PAIR]
    tile_eb = jnp.asarray(PAIR_B)[tile_cls % N_PAIR]

    counts_p = cnt_p[0, :N_CLASS].astype(jnp.int32)
    tile_rows_p = jnp.clip((offsets + counts_p)[tile_cls] - tile_start, 0, TS)
    xs1 = _dispatch_call(pos_p, z1_p, route_p, None, tile_rows_p, name="dispatch_prompt")
    xs2 = _dispatch_call(pos_s, z1_s, route_s, xs1, tile_rows, name="dispatch_sample")
    ys = _moe_call(tile_grp, tile_ea, tile_eb, tile_rows, xs2, wg, wu, wd)
    l2g, l2b = row(ln2_g[l]), row(ln2_b[l])
    y_p = _combine_call(pos_p, z1_p, l2g, l2b, ys, name="combine_prompt")
    y_s = _combine_call(pos_s, z1_s, l2g, l2b, ys, name="combine_sample")

    def heads(a, lead):
        return a.reshape(*lead, N_KV, HEAD_DIM)[None]

    tail = slice(SUBLANE - (CONV_W - 1), SUBLANE)
    y_prompt = y_p.reshape(B, S, D_MODEL)
    y_sample = y_s.reshape(DB, TSQ, D_MODEL)
    meta_k_p = jnp.broadcast_to(heads(k_m, (1, N_META)), (1, B, N_META, N_KV, HEAD_DIM))
    meta_v_p = jnp.broadcast_to(heads(v_m, (1, N_META)), (1, B, N_META, N_KV, HEAD_DIM))
    win_k_p = heads(k_p.reshape(B, S, D_KV)[:, S - WINDOW:], (B, WINDOW))
    win_v_p = heads(v_p.reshape(B, S, D_KV)[:, S - WINDOW:], (B, WINDOW))
    conv_out_p = conv_p[:, tail, :][None]
    h_out_p = h_p[:, 0, :][None]
    new_k_s = heads(k_s, (DB, TSQ))
    new_v_s = heads(v_s, (DB, TSQ))
    conv_out_s = conv_s[:, tail, :][None]
    h_out_s = h_s[:, 0, :][None].astype(state_h.dtype)
    return (y_prompt, y_sample, meta_k_p, meta_v_p, win_k_p, win_v_p, conv_out_p, h_out_p,
            new_k_s, new_v_s, conv_out_s, h_out_s)
```

```python
import functools

import numpy as np
import jax
import jax.numpy as jnp
from jax import lax
from jax.experimental import pallas as pl
from jax.experimental.pallas import tpu as pltpu

F32 = jnp.float32
BF16 = jnp.bfloat16

D_MODEL = 1024
N_META = 16
CHUNK = 64
N_HEADS = 8
N_KV = 2
HEAD_DIM = 64
D_ATTN = N_HEADS * HEAD_DIM
D_KV = N_KV * HEAD_DIM
WINDOW = 128
PAST_LEN = 4096
ROPE_DIM = HEAD_DIM // 4
ROPE_THETA = 500000.0
D_RNN = D_MODEL
N_RNN_BLOCKS = 8
RNN_BLOCK = D_RNN // N_RNN_BLOCKS
CONV_W = 4
LRU_C = 8.0
N_GROUPS = 4
EXPERTS_PER_GROUP = 4
D_EXPERT = 512
DEPTH = 1
ALPHA = (2 * DEPTH) ** 0.25
LN_EPS = 1e-5

C_Q, C_K, C_V = 0, D_ATTN, D_ATTN + D_KV
C_XR = D_ATTN + 2 * D_KV
C_GR = C_XR + D_RNN
C_GL = C_GR + D_RNN
D_IN = C_GL + 2 * D_MODEL

LANE = 128
SUBLANE = 8
TM = 256
TS = 256
DMA_UNROLL = 8
N_PAIR = 6
N_CLASS = N_GROUPS * N_PAIR
XS_W = D_MODEL + LANE
VMEM_LIMIT = 56 * 1024 * 1024

PAIR_A = np.array([0, 0, 0, 1, 1, 2], np.int32)
PAIR_B = np.array([1, 2, 3, 2, 3, 3], np.int32)


def _params(n_grid_dims=1):
    return pltpu.CompilerParams(
        dimension_semantics=("arbitrary",) * n_grid_dims, vmem_limit_bytes=VMEM_LIMIT)


def _layer_norm(x, g, b):
    mu = jnp.mean(x, axis=-1, keepdims=True)
    xc = x - mu
    var = jnp.mean(xc * xc, axis=-1, keepdims=True)
    return xc * lax.rsqrt(var + LN_EPS) * g + b


def _dot(a, b):
    return jnp.dot(a, b, preferred_element_type=F32)


def _sigmoid(x):
    return 0.5 * jnp.tanh(0.5 * x) + 0.5


def _inproj_kernel(x_ref, lng_ref, lnb_ref, win_ref, bgate_ref, rc_ref, rs1_ref, rs2_ref,
                   cw_ref, cb_ref, wrg_ref, ba_ref, bx_ref, lam_ref, iconv_ref, ih_ref,
                   q_ref, k_ref, v_ref, rnn_ref, g_ref, conv_ref, h_ref,
                   xr_s, gr_s, xbuf, abuf, ubuf, hcar, *, seg, nseg, tiles_per_stream):
    i = pl.program_id(0)
    slot_proj = i % 2
    slot_rec = 1 - slot_proj

    @pl.when(i == 0)
    def _():
        xr_s[...] = jnp.zeros_like(xr_s)
        gr_s[...] = jnp.zeros_like(gr_s)
        xbuf[...] = jnp.zeros_like(xbuf)
        hcar[...] = jnp.zeros_like(hcar)

    if tiles_per_stream > 1:
        @pl.when((i > 0) & ((i - 1) % tiles_per_stream == 0))
        def _():
            xbuf[0:SUBLANE, :] = iconv_ref[0]
            hcar[...] = ih_ref[0]

    c8 = LRU_C * jax.nn.softplus(-lam_ref[...])
    row = lax.broadcasted_iota(jnp.int32, (SUBLANE, D_RNN), 0)
    cw = cw_ref[...]
    for s in range(nseg):
        r0 = s * seg
        if tiles_per_stream == 1:
            xbuf[0:SUBLANE, :] = iconv_ref[s]
            hcar[...] = ih_ref[s]

        xbuf[SUBLANE:SUBLANE + seg, :] = xr_s[slot_rec, r0:r0 + seg, :]
        xc = (xbuf[SUBLANE:SUBLANE + seg, :] * cw[3:4, :]
              + xbuf[SUBLANE - 1:SUBLANE - 1 + seg, :] * cw[2:3, :]
              + xbuf[SUBLANE - 2:SUBLANE - 2 + seg, :] * cw[1:2, :]
              + xbuf[SUBLANE - 3:SUBLANE - 3 + seg, :] * cw[0:1, :]
              + cb_ref[...])
        tail = xbuf[seg:seg + SUBLANE, :]
        conv_ref[s] = tail
        xbuf[0:SUBLANE, :] = tail

        for n in range(N_RNN_BLOCKS):
            sl = slice(n * RNN_BLOCK, (n + 1) * RNN_BLOCK)
            xcn = xc[:, sl]
            ri = _dot(xcn.astype(BF16), wrg_ref[n])
            r = _sigmoid(ri[:, :RNN_BLOCK] + ba_ref[:, sl])
            ig = _sigmoid(ri[:, RNN_BLOCK:] + bx_ref[:, sl])
            log_a = -c8[:, sl] * r
            a = jnp.exp(log_a)
            y = -jnp.tanh(log_a) * (a * a + 1.0)
            mult = jnp.where(y > 0.0, y * lax.rsqrt(y), 0.0)
            abuf[0:seg, sl] = a
            ubuf[0:seg, sl] = mult * ig * xcn

        h = hcar[0:1, :]
        for gi in range(seg // SUBLANE):
            o = gi * SUBLANE
            a = abuf[o:o + SUBLANE, :]
            u = ubuf[o:o + SUBLANE, :]
            for sh in (1, 2, 4):
                keep = row >= sh
                u = u + a * jnp.where(keep, pltpu.roll(u, sh, 0), 0.0)
                a = a * jnp.where(keep, pltpu.roll(a, sh, 0), 1.0)
            hg = a * h + u
            ubuf[o:o + SUBLANE, :] = hg
            h = hg[SUBLANE - 1:SUBLANE, :]
        h_b = jnp.broadcast_to(h, (SUBLANE, D_RNN))
        hcar[...] = h_b
        h_ref[s] = h_b
        rnn_ref[r0:r0 + seg, :] = (
            ubuf[0:seg, :] * jax.nn.gelu(gr_s[slot_rec, r0:r0 + seg, :])).astype(BF16)

    zn = _layer_norm(x_ref[...], lng_ref[...], lnb_ref[...]).astype(BF16)
    qkv = _dot(zn, win_ref[:, C_Q:C_XR])
    rc, rs1, rs2 = rc_ref[...], rs1_ref[...], rs2_ref[...]

    def rope(t):
        return t * rc + pltpu.roll(t, 8, 1) * rs1 + pltpu.roll(t, LANE - 8, 1) * rs2

    for j in range(D_ATTN // LANE):
        q_ref[:, j * LANE:(j + 1) * LANE] = (
            rope(qkv[:, j * LANE:(j + 1) * LANE]) * (HEAD_DIM ** -0.5)).astype(BF16)
    k_ref[...] = rope(qkv[:, C_K:C_V])
    v_ref[...] = qkv[:, C_V:C_XR]
    g_ref[...] = _sigmoid(_dot(zn, win_ref[:, C_GL:D_IN]) + bgate_ref[...]).astype(BF16)
    xr_s[slot_proj] = _dot(zn, win_ref[:, C_XR:C_GR])
    gr_s[slot_proj] = _dot(zn, win_ref[:, C_GR:C_GL])


def _inproj_call(x2d, rope_tabs, iconv, ih, wts, *, tm, seg, tiles_per_stream, rope_tiles, name):
    n = x2d.shape[0]
    nt = n // tm
    nseg = tm // seg
    n_streams = n // (seg * tiles_per_stream) if tiles_per_stream > 1 else n // seg
    shared_init = iconv.shape[0] == 1

    def cur(i):
        return jnp.minimum(i, nt - 1)

    def prev(i):
        return jnp.maximum(i - 1, 0)

    def full(a):
        return pl.BlockSpec(a.shape, lambda i: (0,) * a.ndim)

    def rows(w):
        return pl.BlockSpec((tm, w), lambda i: (cur(i), 0))

    rope_spec = pl.BlockSpec((tm, LANE), lambda i: (cur(i) % rope_tiles, 0))
    if shared_init:
        init_spec = pl.BlockSpec((1, SUBLANE, D_RNN), lambda i: (0, 0, 0))
    else:
        init_spec = pl.BlockSpec((nseg, SUBLANE, D_RNN), lambda i: (prev(i), 0, 0))
    state_spec = pl.BlockSpec((nseg, SUBLANE, D_RNN),
                              lambda i: (prev(i) // tiles_per_stream, 0, 0))

    (lng, lnb, win, bgate, cw, cb, wrg, ba, bx, lam) = wts
    kern = functools.partial(_inproj_kernel, seg=seg, nseg=nseg, tiles_per_stream=tiles_per_stream)
    return pl.pallas_call(
        kern,
        grid=(nt + 1,),
        in_specs=[rows(D_MODEL), full(lng), full(lnb), full(win), full(bgate),
                  rope_spec, rope_spec, rope_spec,
                  full(cw), full(cb), full(wrg), full(ba), full(bx), full(lam),
                  init_spec, init_spec],
        out_specs=[rows(D_ATTN), rows(D_KV), rows(D_KV),
                   pl.BlockSpec((tm, D_RNN), lambda i: (prev(i), 0)),
                   rows(2 * D_MODEL), state_spec, state_spec],
        out_shape=[jax.ShapeDtypeStruct((n, D_ATTN), BF16),
                   jax.ShapeDtypeStruct((n, D_KV), F32),
                   jax.ShapeDtypeStruct((n, D_KV), F32),
                   jax.ShapeDtypeStruct((n, D_RNN), BF16),
                   jax.ShapeDtypeStruct((n, 2 * D_MODEL), BF16),
                   jax.ShapeDtypeStruct((n_streams, SUBLANE, D_RNN), F32),
                   jax.ShapeDtypeStruct((n_streams, SUBLANE, D_RNN), F32)],
        scratch_shapes=[pltpu.VMEM((2, tm, D_RNN), F32),
                        pltpu.VMEM((2, tm, D_RNN), F32),
                        pltpu.VMEM((SUBLANE + seg, D_RNN), F32),
                        pltpu.VMEM((seg, D_RNN), F32),
                        pltpu.VMEM((seg, D_RNN), F32),
                        pltpu.VMEM((SUBLANE, D_RNN), F32)],
        compiler_params=_params(),
        name=name,
    )(x2d, lng, lnb, win, bgate, *rope_tabs, cw, cb, wrg, ba, bx, lam, iconv, ih)


N_KEYS = N_META + WINDOW + CHUNK
KEY_PAD = 2 * LANE


def _attn_kernel(sink_ref, q_ref, km_ref, vm_ref, kp_ref, vp_ref, ko_ref, vo_ref, o_ref,
                 *, nchunk, tiles_per_stream, mask_first):
    i = pl.program_id(0)
    first = (i % tiles_per_stream) == 0
    lane = lax.broadcasted_iota(jnp.int32, (1, LANE), 1)
    lo = lane < HEAD_DIM

    def slabs(x):
        xs = pltpu.roll(x, HEAD_DIM, 1)
        return (jnp.where(lo, x, xs).astype(BF16), jnp.where(lo, xs, x).astype(BF16))

    km, vm = slabs(km_ref[...]), slabs(vm_ref[...])
    kp, vp = slabs(kp_ref[...]), slabs(vp_ref[...])
    ko, vo = slabs(ko_ref[...]), slabs(vo_ref[...])
    zpad = jnp.zeros((KEY_PAD - N_KEYS, LANE), BF16)
    kpos = lax.broadcasted_iota(jnp.int32, (1, KEY_PAD), 1)
    qrow = lax.broadcasted_iota(jnp.int32, (4 * CHUNK, 1), 0)
    q_per_kv = N_HEADS // N_KV

    def window(prev, own, c):
        parts = []
        if CHUNK * c < WINDOW:
            parts.append(prev[CHUNK * c:WINDOW])
        parts.append(own[max(CHUNK * c - WINDOW, 0):CHUNK * (c + 1)])
        return parts

    for c in range(nchunk):
        qc = q_ref[c * CHUNK:(c + 1) * CHUNK, :]
        for kv in range(N_KV):
            kc = jnp.concatenate([km[kv]] + window(kp[kv], ko[kv], c) + [zpad], axis=0)
            vc = jnp.concatenate([vm[kv]] + window(vp[kv], vo[kv], c) + [zpad], axis=0)
            qs = []
            for g in range(q_per_kv):
                h = kv * q_per_kv + g
                slab = qc[:, (h // 2) * LANE:(h // 2 + 1) * LANE]
                keep = lo if h % 2 == 0 else jnp.logical_not(lo)
                qs.append(jnp.where(keep, slab, jnp.zeros_like(slab)))
            qst = jnp.concatenate(qs, axis=0)
            s = lax.dot_general(qst, kc, (((1,), (1,)), ((), ())), preferred_element_type=F32)
            bad = kpos >= N_KEYS
            n_masked = WINDOW - CHUNK * c
            if mask_first and n_masked > 0:
                lim = N_META + jnp.where(first, n_masked, 0)
                bad = bad | ((kpos >= N_META) & (kpos < lim))
            s = jnp.where(bad, -jnp.inf, s)
            sink = jnp.full((4 * CHUNK, 1), sink_ref[kv * q_per_kv + q_per_kv - 1], F32)
            for g in range(q_per_kv - 2, -1, -1):
                sink = jnp.where(qrow < (g + 1) * CHUNK, sink_ref[kv * q_per_kv + g], sink)
            m = jnp.maximum(jnp.max(s, axis=-1, keepdims=True), sink)
            p = jnp.exp(s - m)
            denom = jnp.sum(p, axis=-1, keepdims=True) + jnp.exp(sink - m)
            o = _dot(p.astype(BF16), vc) / denom
            for jj in range(q_per_kv // 2):
                g0 = 2 * jj
                oslab = jnp.where(lo, o[g0 * CHUNK:(g0 + 1) * CHUNK, :],
                                  o[(g0 + 1) * CHUNK:(g0 + 2) * CHUNK, :])
                j = kv * (q_per_kv // 2) + jj
                o_ref[c * CHUNK:(c + 1) * CHUNK, j * LANE:(j + 1) * LANE] = oslab.astype(BF16)


def _attn_call(sinks, q, kmeta, vmeta, kprev, vprev, kown, vown, *, tq, tiles_per_stream,
               mask_first, prev_map, meta_map, name):
    n = q.shape[0]
    nt = n // tq
    kern = functools.partial(_attn_kernel, nchunk=tq // CHUNK, tiles_per_stream=tiles_per_stream,
                             mask_first=mask_first)
    meta_spec = pl.BlockSpec((N_META, D_KV), meta_map)
    prev_spec = pl.BlockSpec((WINDOW, D_KV), prev_map)
    own_spec = pl.BlockSpec((tq, D_KV), lambda i: (i, 0))
    return pl.pallas_call(
        kern,
        grid=(nt,),
        in_specs=[pl.BlockSpec(memory_space=pltpu.SMEM),
                  pl.BlockSpec((tq, D_ATTN), lambda i: (i, 0)),
                  meta_spec, meta_spec, prev_spec, prev_spec, own_spec, own_spec],
        out_specs=pl.BlockSpec((tq, D_ATTN), lambda i: (i, 0)),
        out_shape=jax.ShapeDtypeStruct((n, D_ATTN), BF16),
        compiler_params=_params(),
        name=name,
    )(sinks, q, kmeta, vmeta, kprev, vprev, kown, vown)


def _merge_kernel(o_ref, rnn_ref, g_ref, x_ref, lng_ref, lnb_ref, wpa_ref, wpr_ref, wo_ref,
                  l1g_ref, l1b_ref, wr_ref, br_ref, cin_ref,
                  z1_ref, route_ref, slot_ref, cout_ref, cnt):
    i = pl.program_id(0)

    @pl.when(i == 0)
    def _():
        cnt[...] = cin_ref[...]

    g = g_ref[...]
    pa = _dot(o_ref[...], wpa_ref[...])
    pr = _dot(rnn_ref[...], wpr_ref[...])
    mixed = g[:, :D_MODEL].astype(F32) * pa + g[:, D_MODEL:].astype(F32) * pr
    mix = _dot(mixed.astype(BF16), wo_ref[...])
    zn = _layer_norm(x_ref[...], lng_ref[...], lnb_ref[...])
    z1 = _layer_norm(ALPHA * zn + mix, l1g_ref[...], l1b_ref[...])
    z1_ref[...] = z1

    z_hi = z1.astype(BF16)
    z_lo = (z1 - z_hi.astype(F32)).astype(BF16)
    la = _dot(z_hi, wr_ref[...])
    lb = _dot(z_lo, wr_ref[...])
    logit = la[:, :LANE] + la[:, LANE:] + lb[:, :LANE] + lb[:, LANE:] + br_ref[...]

    tm = logit.shape[0]
    lane = lax.broadcasted_iota(jnp.int32, (tm, LANE), 1).astype(F32)
    ninf = -jnp.inf
    big = float(LANE)
    gl = jnp.where(lane < N_GROUPS, logit, ninf)
    gmax = jnp.max(gl, axis=-1, keepdims=True)
    gidx = jnp.min(jnp.where(gl == gmax, lane, big), axis=-1, keepdims=True)
    gw = 1.0 / jnp.sum(jnp.exp(gl - gmax), axis=-1, keepdims=True)
    lane_grp = jnp.floor((lane - N_GROUPS) * (1.0 / EXPERTS_PER_GROUP))
    el = jnp.where((lane >= N_GROUPS) & (lane_grp == gidx), logit, ninf)
    v1 = jnp.max(el, axis=-1, keepdims=True)
    i1 = jnp.min(jnp.where(el == v1, lane, big), axis=-1, keepdims=True)
    el2 = jnp.where(lane == i1, ninf, el)
    v2 = jnp.max(el2, axis=-1, keepdims=True)
    i2 = jnp.min(jnp.where(el2 == v2, lane, big), axis=-1, keepdims=True)
    e21 = jnp.exp(v2 - v1)
    w1 = gw / (1.0 + e21)
    w2 = gw * e21 / (1.0 + e21)
    base = N_GROUPS + EXPERTS_PER_GROUP * gidx
    l1, l2 = i1 - base, i2 - base
    first_lo = l1 < l2
    ea, eb = jnp.minimum(l1, l2), jnp.maximum(l1, l2)
    wa = jnp.where(first_lo, w1, w2)
    wb = jnp.where(first_lo, w2, w1)
    pair = ea * (7.0 - ea) * 0.5 + (eb - ea - 1.0)
    cls = gidx * N_PAIR + pair

    onehot = lane == cls
    r_i = lax.broadcasted_iota(jnp.int32, (tm, tm), 0)
    c_i = lax.broadcasted_iota(jnp.int32, (tm, tm), 1)
    tri = jnp.where(c_i < r_i, 1.0, 0.0).astype(BF16)
    before = _dot(tri, jnp.where(onehot, 1.0, 0.0).astype(BF16))
    ohf = jnp.where(onehot, 1.0, 0.0)
    rank = jnp.sum(ohf * (before + cnt[0:1, :]), axis=-1, keepdims=True)
    cnt[...] = cnt[...] + jnp.sum(ohf, axis=0, keepdims=True)
    cout_ref[...] = cnt[...]

    eye = r_i == c_i
    cls_row = jnp.sum(jnp.where(eye, cls, 0.0), axis=0, keepdims=True)
    rank_row = jnp.sum(jnp.where(eye, rank, 0.0), axis=0, keepdims=True)
    sub = lax.broadcasted_iota(jnp.int32, (SUBLANE, tm), 0)
    slot_ref[...] = jnp.where(sub == 0, cls_row, jnp.where(sub == 1, rank_row, 0.0))

    route_ref[...] = jnp.where(lane == 0.0, cls,
                     jnp.where(lane == 1.0, wa,
                     jnp.where(lane == 2.0, wb,
                     jnp.where(lane == 3.0, rank, 0.0))))


def _merge_call(o, rnn, g, x2d, counts_in, wts, *, name):
    n = x2d.shape[0]
    nt = n // TM

    def full(a):
        return pl.BlockSpec(a.shape, lambda i: (0,) * a.ndim)

    def rows(w):
        return pl.BlockSpec((TM, w), lambda i: (i, 0))

    return pl.pallas_call(
        _merge_kernel,
        grid=(nt,),
        in_specs=[rows(D_ATTN), rows(D_RNN), rows(2 * D_MODEL), rows(D_MODEL)]
                 + [full(w) for w in wts] + [full(counts_in)],
        out_specs=[rows(D_MODEL), rows(LANE),
                   pl.BlockSpec((None, SUBLANE, TM), lambda i: (i, 0, 0)),
                   pl.BlockSpec((SUBLANE, LANE), lambda i: (0, 0))],
        out_shape=[jax.ShapeDtypeStruct((n, D_MODEL), F32),
                   jax.ShapeDtypeStruct((n, LANE), F32),
                   jax.ShapeDtypeStruct((nt, SUBLANE, TM), F32),
                   jax.ShapeDtypeStruct((SUBLANE, LANE), F32)],
        scratch_shapes=[pltpu.VMEM((SUBLANE, LANE), F32)],
        compiler_params=_params(),
        name=name,
    )(o, rnn, g, x2d, *wts, counts_in)


def _dispatch_kernel(pos_ref, z1_ref, route_ref, extra_ref, xs_ref, rowbuf, sem, *, nt, fresh):
    i = pl.program_id(0)
    slot = i % 2

    def row_copy(sl, r, p):
        return pltpu.make_async_copy(rowbuf.at[sl, pl.ds(r, 1), :], xs_ref.at[pl.ds(p, 1), :],
                                     sem.at[sl])

    def tile_copy(sl, t):
        return pltpu.make_async_copy(rowbuf.at[sl], xs_ref.at[pl.ds(t * TS, TS), :], sem.at[sl])

    def drain(sl):
        tile_copy(sl, 0).wait()

    if fresh:
        @pl.when(i == 0)
        def _():
            rowbuf[1] = jnp.zeros((TM, XS_W), F32)
            n_tiles = extra_ref.shape[0]

            def fill(t, c):
                @pl.when(extra_ref[t] < TS)
                def _():
                    tile_copy(1, t).start()
                return c
            lax.fori_loop(0, n_tiles, fill, 0)

            def fill_wait(t, c):
                @pl.when(extra_ref[t] < TS)
                def _():
                    tile_copy(1, 0).wait()
                return c
            lax.fori_loop(0, n_tiles, fill_wait, 0)

    @pl.when(i >= 2)
    def _():
        drain(slot)

    rowbuf[slot, :, 0:D_MODEL] = z1_ref[...]
    rowbuf[slot, :, D_MODEL:XS_W] = route_ref[...]

    def issue(rb, c):
        for u in range(DMA_UNROLL):
            r = rb * DMA_UNROLL + u
            row_copy(slot, r, pos_ref[0, r]).start(priority=u % 2)
        return c
    lax.fori_loop(0, TM // DMA_UNROLL, issue, 0)

    @pl.when(i == nt - 1)
    def _():
        drain(slot)
        if nt > 1:
            drain(1 - slot)


def _dispatch_call(pos3, z1, route, xs, tile_rows, *, name):
    n = z1.shape[0]
    nt = n // TM
    fresh = xs is None
    n_slots = tile_rows.shape[0] * TS if fresh else xs.shape[0]
    extra = tile_rows if fresh else xs
    extra_spec = pl.BlockSpec(memory_space=pltpu.SMEM if fresh else pl.ANY)
    return pl.pallas_call(
        functools.partial(_dispatch_kernel, nt=nt, fresh=fresh),
        grid=(nt,),
        in_specs=[pl.BlockSpec((None, 1, TM), lambda i: (i, 0, 0), memory_space=pltpu.SMEM),
                  pl.BlockSpec((TM, D_MODEL), lambda i: (i, 0)),
                  pl.BlockSpec((TM, LANE), lambda i: (i, 0)),
                  extra_spec],
        out_specs=pl.BlockSpec(memory_space=pl.ANY),
        out_shape=jax.ShapeDtypeStruct((n_slots, XS_W), F32),
        input_output_aliases={} if fresh else {3: 0},
        scratch_shapes=[pltpu.VMEM((2, TM, XS_W), F32), pltpu.SemaphoreType.DMA((2,))],
        compiler_params=_params(),
        name=name,
    )(pos3, z1, route, extra)


def _moe_kernel(grp_ref, ea_ref, eb_ref, val_ref, xs_ref, wg_ref, wu_ref, wd_ref, y_ref):
    del grp_ref
    i = pl.program_id(0)

    @pl.when(val_ref[i] > 0)
    def _():
        ea = ea_ref[i]
        eb = eb_ref[i]
        xb = xs_ref[:, 0:D_MODEL].astype(BF16)
        scal = xs_ref[:, D_MODEL:XS_W]

        def hidden(e, w):
            h = jax.nn.silu(_dot(xb, wg_ref[e])) * _dot(xb, wu_ref[e])
            return (h * w).astype(BF16)

        ha = hidden(ea, scal[:, 1:2])
        hb = hidden(eb, scal[:, 2:3])
        y_ref[...] = _dot(ha, wd_ref[ea]) + _dot(hb, wd_ref[eb])

    @pl.when(val_ref[i] == 0)
    def _():
        y_ref[...] = jnp.zeros_like(y_ref)


def _moe_call(tile_grp, tile_ea, tile_eb, tile_valid, xs, wg, wu, wd):
    n_tiles = xs.shape[0] // TS
    wspec_in = pl.BlockSpec((None, EXPERTS_PER_GROUP, D_MODEL, D_EXPERT),
                            lambda i, grp, ea, eb, val: (grp[i], 0, 0, 0))
    wspec_out = pl.BlockSpec((None, EXPERTS_PER_GROUP, D_EXPERT, D_MODEL),
                             lambda i, grp, ea, eb, val: (grp[i], 0, 0, 0))
    grid_spec = pltpu.PrefetchScalarGridSpec(
        num_scalar_prefetch=4,
        grid=(n_tiles,),
        in_specs=[pl.BlockSpec((TS, XS_W), lambda i, grp, ea, eb, val: (i, 0)),
                  wspec_in, wspec_in, wspec_out],
        out_specs=pl.BlockSpec((TS, D_MODEL), lambda i, grp, ea, eb, val: (i, 0)),
    )
    return pl.pallas_call(
        _moe_kernel,
        grid_spec=grid_spec,
        out_shape=jax.ShapeDtypeStruct((xs.shape[0], D_MODEL), F32),
        compiler_params=_params(),
        name="moe_experts",
    )(tile_grp, tile_ea, tile_eb, tile_valid, xs, wg, wu, wd)


def _combine_kernel(pos_ref, posn_ref, z1_ref, l2g_ref, l2b_ref, ys_ref, out_ref, ybuf, sem, *, nt):
    i = pl.program_id(0)
    slot = i % 2

    def row_copy(sl, r, p):
        return pltpu.make_async_copy(ys_ref.at[pl.ds(p, 1), :], ybuf.at[sl, pl.ds(r, 1), :],
                                     sem.at[sl])

    def issue(sl, pref):
        def body(rb, c):
            for u in range(DMA_UNROLL):
                r = rb * DMA_UNROLL + u
                row_copy(sl, r, pref[0, r]).start(priority=u % 2)
            return c
        lax.fori_loop(0, TM // DMA_UNROLL, body, 0)

    @pl.when(i == 0)
    def _():
        issue(0, pos_ref)

    @pl.when(i + 1 < nt)
    def _():
        issue(1 - slot, posn_ref)

    pltpu.make_async_copy(ys_ref.at[pl.ds(0, TM), :], ybuf.at[slot], sem.at[slot]).wait()

    out_ref[...] = _layer_norm(ALPHA * z1_ref[...] + ybuf[slot], l2g_ref[...], l2b_ref[...])


def _combine_call(pos3, z1, l2g, l2b, ys, *, name):
    n = z1.shape[0]
    nt = n // TM
    return pl.pallas_call(
        functools.partial(_combine_kernel, nt=nt),
        grid=(nt,),
        in_specs=[pl.BlockSpec((None, 1, TM), lambda i: (i, 0, 0), memory_space=pltpu.SMEM),
                  pl.BlockSpec((None, 1, TM), lambda i: (jnp.minimum(i + 1, nt - 1), 0, 0),
                               memory_space=pltpu.SMEM),
                  pl.BlockSpec((TM, D_MODEL), lambda i: (i, 0)),
                  pl.BlockSpec(l2g.shape, lambda i: (0, 0)),
                  pl.BlockSpec(l2b.shape, lambda i: (0, 0)),
                  pl.BlockSpec(memory_space=pl.ANY)],
        out_specs=pl.BlockSpec((TM, D_MODEL), lambda i: (i, 0)),
        out_shape=jax.ShapeDtypeStruct((n, D_MODEL), F32),
        scratch_shapes=[pltpu.VMEM((2, TM, D_MODEL), F32), pltpu.SemaphoreType.DMA((2,))],
        compiler_params=_params(),
        name=name,
    )(pos3, pos3, z1, l2g, l2b, ys)


def _rope_tables(pos, reps=1):
    half = ROPE_DIM // 2
    inv_freq = ROPE_THETA ** (-np.arange(half, dtype=np.float64) / half)
    ang = pos.astype(np.float64)[:, None] * inv_freq[None, :]
    cos, sin = np.cos(ang), np.sin(ang)
    p = pos.shape[0]
    ones = np.ones((p, HEAD_DIM - ROPE_DIM))
    zeros = np.zeros((p, HEAD_DIM - ROPE_DIM))
    zh = np.zeros((p, half))
    c = np.concatenate([cos, cos, ones], axis=1)
    s1 = np.concatenate([zh, sin, zeros], axis=1)
    s2 = np.concatenate([-sin, zh, zeros], axis=1)
    rep = LANE // HEAD_DIM
    return tuple(jnp.asarray(np.tile(t, (reps, rep)).astype(np.float32)) for t in (c, s1, s2))


def kernel(x_prompt, x_sample, cache_meta_k, cache_meta_v, cache_win_k, cache_win_v, state_conv, state_h, meta_tokens, ln_in_g, ln_in_b, w_in, b_gate, attn_sinks, conv_w, conv_b, w_rg_a, b_rg_a, w_rg_x, b_rg_x, lru_lambda, w_branch_attn, w_branch_rnn, w_out, ln1_g, ln1_b, w_group, b_group, w_router, b_router, w_gate, w_up, w_down, ln2_g, ln2_b):
    B, S, _ = x_prompt.shape
    DB, TSQ, _ = x_sample.shape
    assert DEPTH == 1 and w_in.shape == (DEPTH, D_MODEL, D_IN)
    assert S % TM == 0 and (DB * TSQ) % TM == 0 and TM % TSQ == 0 and TSQ == CHUNK
    assert cache_win_k.shape[2] == WINDOW and S >= WINDOW
    l = 0

    def row(v):
        return v.reshape(1, -1).astype(F32)

    lng, lnb = row(ln_in_g), row(ln_in_b)
    win = w_in[l].astype(BF16)
    cw = jnp.concatenate([conv_w[l], jnp.zeros((SUBLANE - CONV_W, D_RNN), F32)], axis=0)
    wrg = jnp.concatenate([w_rg_a[l], w_rg_x[l]], axis=-1).astype(BF16)
    in_wts = (lng, lnb, win, row(b_gate[l]), cw, row(conv_b[l]), wrg,
              row(b_rg_a[l]), row(b_rg_x[l]), row(lru_lambda[l]))
    wr = jnp.concatenate([w_group[l], w_router[l].reshape(D_MODEL, -1)], axis=1)
    wr = jnp.pad(wr, ((0, 0), (0, LANE - wr.shape[1])))
    wr_hi = wr.astype(BF16)
    wr_lo = (wr - wr_hi.astype(F32)).astype(BF16)
    wr_cat = jnp.concatenate([wr_hi, wr_lo], axis=1)
    br = jnp.pad(jnp.concatenate([b_group[l], b_router[l].reshape(-1)]),
                 (0, LANE - N_GROUPS * (1 + EXPERTS_PER_GROUP))).reshape(1, LANE)
    merge_wts = (lng, lnb, w_branch_attn[l].astype(BF16), w_branch_rnn[l].astype(BF16),
                 w_out[l].astype(BF16), row(ln1_g[l]), row(ln1_b[l]), wr_cat, br)
    wg = w_gate[l].astype(BF16).reshape(N_GROUPS, EXPERTS_PER_GROUP, D_MODEL, D_EXPERT)
    wu = w_up[l].astype(BF16).reshape(N_GROUPS, EXPERTS_PER_GROUP, D_MODEL, D_EXPERT)
    wd = w_down[l].astype(BF16).reshape(N_GROUPS, EXPERTS_PER_GROUP, D_EXPERT, D_MODEL)
    sinks = attn_sinks[l].astype(F32)

    rope_meta = _rope_tables(np.arange(N_META))
    rope_p = _rope_tables(N_META + np.arange(S))
    rope_s = _rope_tables(N_META + PAST_LEN + np.arange(TSQ), reps=TM // TSQ)

    xp = x_prompt.reshape(B * S, D_MODEL)
    xs_ = x_sample.reshape(DB * TSQ, D_MODEL)
    n_p, n_s = B * S, DB * TSQ
    zero_state = jnp.zeros((1, SUBLANE, D_RNN), F32)

    _, k_m, v_m, _, _, conv_m, h_m = _inproj_call(
        meta_tokens.astype(F32), rope_meta, zero_state, zero_state, in_wts,
        tm=N_META, seg=N_META, tiles_per_stream=1, rope_tiles=1, name="inproj_meta")
    q_p, k_p, v_p, rnn_p, g_p, conv_p, h_p = _inproj_call(
        xp, rope_p, conv_m, h_m, in_wts,
        tm=TM, seg=TM, tiles_per_stream=S // TM, rope_tiles=S // TM, name="inproj_prompt")
    iconv_s = jnp.pad(state_conv[l].astype(F32), ((0, 0), (SUBLANE - (CONV_W - 1), 0), (0, 0)))
    ih_s = jnp.broadcast_to(state_h[l].astype(F32)[:, None, :], (DB, SUBLANE, D_RNN))
    q_s, k_s, v_s, rnn_s, g_s, conv_s, h_s = _inproj_call(
        xs_, rope_s, iconv_s, ih_s, in_wts,
        tm=TM, seg=TSQ, tiles_per_stream=1, rope_tiles=1, name="inproj_sample")

    tps = S // TM
    win_blocks = TM // WINDOW
    o_p = _attn_call(
        sinks, q_p, k_m, v_m, k_p, v_p, k_p, v_p, tq=TM, tiles_per_stream=tps, mask_first=True,
        prev_map=lambda i: (jnp.maximum(i * win_blocks - 1, 0), 0),
        meta_map=lambda i: (0, 0), name="attn_prompt")
    cmk = cache_meta_k[l].reshape(DB * N_META, D_KV).astype(F32)
    cmv = cache_meta_v[l].reshape(DB * N_META, D_KV).astype(F32)
    cwk = cache_win_k[l].reshape(DB * WINDOW, D_KV).astype(F32)
    cwv = cache_win_v[l].reshape(DB * WINDOW, D_KV).astype(F32)
    o_s = _attn_call(
        sinks, q_s, cmk, cmv, cwk, cwv, k_s, v_s, tq=TSQ, tiles_per_stream=1, mask_first=False,
        prev_map=lambda i: (i, 0), meta_map=lambda i: (i, 0), name="attn_sample")

    zero_cnt = jnp.zeros((SUBLANE, LANE), F32)
    z1_p, route_p, slot_p, cnt_p = _merge_call(
        o_p, rnn_p, g_p, xp, zero_cnt, merge_wts, name="merge_prompt")
    z1_s, route_s, slot_s, cnt_all = _merge_call(
        o_s, rnn_s, g_s, xs_, cnt_p, merge_wts, name="merge_sample")

    counts = cnt_all[0, :N_CLASS].astype(jnp.int32)
    padded = ((counts + TS - 1) // TS) * TS
    ends = jnp.cumsum(padded)
    offsets = ends - padded
    n_tiles = (n_p + n_s) // TS + N_CLASS

    def lookup(table, idx):
        out = jnp.zeros(idx.shape, table.dtype)
        for c in range(table.shape[0]):
            out = jnp.where(idx == c, table[c], out)
        return out

    def slots(slot_rows):
        cls = slot_rows[:, 0:1, :].astype(jnp.int32)
        return lookup(offsets, cls) + slot_rows[:, 1:2, :].astype(jnp.int32)

    pos_p, pos_s = slots(slot_p), slots(slot_s)
    tile_start = jnp.arange(n_tiles, dtype=jnp.int32) * TS
    last_cls = jnp.sum((ends <= ends[-1] - 1).astype(jnp.int32))
    tile_cls = jnp.minimum(jnp.sum((tile_start[:, None] >= ends[None, :]).astype(jnp.int32), axis=1),
                           jnp.minimum(last_cls, N_CLASS - 1))
    tile_rows = jnp.clip(lookup(offsets + counts, tile_cls) - tile_start, 0, TS)
    tile_grp = tile_cls // N_PAIR
    tile_ea = lookup(jnp.asarray(PAIR_A), tile_cls % N_PAIR)
    tile_eb = lookup(jnp.asarray(PAIR_B), tile_cls % N_PAIR)

    counts_p = cnt_p[0, :N_CLASS].astype(jnp.int32)
    tile_rows_p = jnp.clip(lookup(offsets + counts_p, tile_cls) - tile_start, 0, TS)
    xs1 = _dispatch_call(pos_p, z1_p, route_p, None, tile_rows_p, name="dispatch_prompt")
    xs2 = _dispatch_call(pos_s, z1_s, route_s, xs1, tile_rows, name="dispatch_sample")
    ys = _moe_call(tile_grp, tile_ea, tile_eb, tile_rows, xs2, wg, wu, wd)
    l2g, l2b = row(ln2_g[l]), row(ln2_b[l])
    y_p = _combine_call(pos_p, z1_p, l2g, l2b, ys, name="combine_prompt")
    y_s = _combine_call(pos_s, z1_s, l2g, l2b, ys, name="combine_sample")

    def heads(a, lead):
        return a.reshape(*lead, N_KV, HEAD_DIM)[None]

    tail = slice(SUBLANE - (CONV_W - 1), SUBLANE)
    y_prompt = y_p.reshape(B, S, D_MODEL)
    y_sample = y_s.reshape(DB, TSQ, D_MODEL)
    meta_k_p = jnp.broadcast_to(heads(k_m, (1, N_META)), (1, B, N_META, N_KV, HEAD_DIM))
    meta_v_p = jnp.broadcast_to(heads(v_m, (1, N_META)), (1, B, N_META, N_KV, HEAD_DIM))
    win_k_p = heads(k_p.reshape(B, S, D_KV)[:, S - WINDOW:], (B, WINDOW))
    win_v_p = heads(v_p.reshape(B, S, D_KV)[:, S - WINDOW:], (B, WINDOW))
    conv_out_p = conv_p[:, tail, :][None]
    h_out_p = h_p[:, 0, :][None]
    new_k_s = heads(k_s, (DB, TSQ))
    new_v_s = heads(v_s, (DB, TSQ))
    conv_out_s = conv_s[:, tail, :][None]
    h_out_s = h_s[:, 0, :][None].astype(state_h.dtype)
    return (y_prompt, y_sample, meta_k_p, meta_v_p, win_k_p, win_v_p, conv_out_p, h_out_p,
            new_k_s, new_v_s, conv_out_s, h_out_s)
```

```python
import functools

import numpy as np
import jax
import jax.numpy as jnp
from jax import lax
from jax.experimental import pallas as pl
from jax.experimental.pallas import tpu as pltpu

F32 = jnp.float32
BF16 = jnp.bfloat16

D_MODEL = 1024
N_META = 16
CHUNK = 64
N_HEADS = 8
N_KV = 2
HEAD_DIM = 64
D_ATTN = N_HEADS * HEAD_DIM
D_KV = N_KV * HEAD_DIM
WINDOW = 128
PAST_LEN = 4096
ROPE_DIM = HEAD_DIM // 4
ROPE_THETA = 500000.0
D_RNN = D_MODEL
N_RNN_BLOCKS = 8
RNN_BLOCK = D_RNN // N_RNN_BLOCKS
CONV_W = 4
LRU_C = 8.0
N_GROUPS = 4
EXPERTS_PER_GROUP = 4
D_EXPERT = 512
DEPTH = 1
ALPHA = (2 * DEPTH) ** 0.25
LN_EPS = 1e-5

C_Q, C_K, C_V = 0, D_ATTN, D_ATTN + D_KV
C_XR = D_ATTN + 2 * D_KV
C_GR = C_XR + D_RNN
C_GL = C_GR + D_RNN
D_IN = C_GL + 2 * D_MODEL

LANE = 128
SUBLANE = 8
TM = 512
TM_SAMPLE = 256
TD = 1024
TS = 256
DMA_UNROLL = 8
N_PAIR = 6
N_CLASS = N_GROUPS * N_PAIR
ROUTE_ROWS = 32
assert N_GROUPS == 4 and EXPERTS_PER_GROUP == 4 and N_CLASS <= ROUTE_ROWS
ROW_TILE = D_MODEL // LANE
PITCH = ROW_TILE + 1
VMEM_LIMIT = 56 * 1024 * 1024

PAIR_A = np.array([0, 0, 0, 1, 1, 3], np.int32)
PAIR_B = np.array([1, 2, 3, 3, 2, 2], np.int32)


def _params(n_grid_dims=1):
    return pltpu.CompilerParams(
        dimension_semantics=("arbitrary",) * n_grid_dims, vmem_limit_bytes=VMEM_LIMIT)


def _layer_norm(x, g, b):
    mu = jnp.mean(x, axis=-1, keepdims=True)
    xc = x - mu
    var = jnp.mean(xc * xc, axis=-1, keepdims=True)
    return xc * lax.rsqrt(var + LN_EPS) * g + b


def _dot(a, b):
    return jnp.dot(a, b, preferred_element_type=F32)


def _store_token_rows(ref, x, scalars):
    n = x.shape[0]
    for j in range(ROW_TILE):
        ref[pl.ds(j, n, stride=PITCH), :] = x[:, j * LANE:(j + 1) * LANE]
    ref[pl.ds(ROW_TILE, n, stride=PITCH), :] = scalars


def _load_token_rows(ref, n):
    return jnp.concatenate(
        [ref[pl.ds(j, n, stride=PITCH), :] for j in range(ROW_TILE)], axis=1)


def _sigmoid(x):
    return 0.5 * jnp.tanh(0.5 * x) + 0.5


def _inproj_kernel(x_ref, lng_ref, lnb_ref, win_ref, bgate_ref, rc_ref, rs1_ref, rs2_ref,
                   cw_ref, cb_ref, wrg_ref, ba_ref, bx_ref, lam_ref, iconv_ref, ih_ref,
                   q_ref, k_ref, v_ref, rnn_ref, g_ref, conv_ref, h_ref,
                   xr_s, gr_s, xbuf, abuf, ubuf, hcar, *, seg, nseg, tiles_per_stream):
    i = pl.program_id(0)
    slot_proj = i % 2
    slot_rec = 1 - slot_proj

    @pl.when(i == 0)
    def _():
        xr_s[...] = jnp.zeros_like(xr_s)
        gr_s[...] = jnp.zeros_like(gr_s)
        xbuf[...] = jnp.zeros_like(xbuf)
        hcar[...] = jnp.zeros_like(hcar)

    if tiles_per_stream > 1:
        @pl.when((i > 0) & ((i - 1) % tiles_per_stream == 0))
        def _():
            xbuf[0:SUBLANE, :] = iconv_ref[0]
            hcar[...] = ih_ref[0]

    c8 = LRU_C * jax.nn.softplus(-lam_ref[...])
    row = lax.broadcasted_iota(jnp.int32, (SUBLANE, D_RNN), 0)
    cw = cw_ref[...]
    for s in range(nseg):
        r0 = s * seg
        if tiles_per_stream == 1:
            xbuf[0:SUBLANE, :] = iconv_ref[s]
            hcar[...] = ih_ref[s]

        xbuf[SUBLANE:SUBLANE + seg, :] = xr_s[slot_rec, r0:r0 + seg, :]
        xc = (xbuf[SUBLANE:SUBLANE + seg, :] * cw[3:4, :]
              + xbuf[SUBLANE - 1:SUBLANE - 1 + seg, :] * cw[2:3, :]
              + xbuf[SUBLANE - 2:SUBLANE - 2 + seg, :] * cw[1:2, :]
              + xbuf[SUBLANE - 3:SUBLANE - 3 + seg, :] * cw[0:1, :]
              + cb_ref[...])
        tail = xbuf[seg:seg + SUBLANE, :]
        conv_ref[s] = tail
        xbuf[0:SUBLANE, :] = tail

        xcb = xc.astype(BF16)
        blocks = [slice(n * RNN_BLOCK, (n + 1) * RNN_BLOCK) for n in range(N_RNN_BLOCKS)]
        gates = [_dot(xcb[:, sl], wrg_ref[n]) for n, sl in enumerate(blocks)]
        for sl, ri in zip(blocks, gates):
            xcn = xc[:, sl]
            r = _sigmoid(ri[:, :RNN_BLOCK] + ba_ref[:, sl])
            ig = _sigmoid(ri[:, RNN_BLOCK:] + bx_ref[:, sl])
            log_a = -c8[:, sl] * r
            a = jnp.exp(log_a)
            y = -jnp.tanh(log_a) * (a * a + 1.0)
            mult = jnp.where(y > 0.0, y * lax.rsqrt(y), 0.0)
            abuf[0:seg, sl] = a
            ubuf[0:seg, sl] = mult * ig * xcn

        h = hcar[0:1, :]
        for gi in range(seg // SUBLANE):
            o = gi * SUBLANE
            a = abuf[o:o + SUBLANE, :]
            u = ubuf[o:o + SUBLANE, :]
            for sh in (1, 2, 4):
                keep = row >= sh
                u = u + a * jnp.where(keep, pltpu.roll(u, sh, 0), 0.0)
                a = a * jnp.where(keep, pltpu.roll(a, sh, 0), 1.0)
            hg = a * h + u
            ubuf[o:o + SUBLANE, :] = hg
            h = hg[SUBLANE - 1:SUBLANE, :]
        h_b = jnp.broadcast_to(h, (SUBLANE, D_RNN))
        hcar[...] = h_b
        h_ref[s] = h_b
        rnn_ref[r0:r0 + seg, :] = (
            ubuf[0:seg, :] * jax.nn.gelu(gr_s[slot_rec, r0:r0 + seg, :])).astype(BF16)

    zn = _layer_norm(x_ref[...], lng_ref[...], lnb_ref[...]).astype(BF16)
    qkv = _dot(zn, win_ref[:, C_Q:C_XR])
    rc, rs1, rs2 = rc_ref[...], rs1_ref[...], rs2_ref[...]

    def rope(t):
        return t * rc + pltpu.roll(t, 8, 1) * rs1 + pltpu.roll(t, LANE - 8, 1) * rs2

    for j in range(D_ATTN // LANE):
        q_ref[:, j * LANE:(j + 1) * LANE] = (
            rope(qkv[:, j * LANE:(j + 1) * LANE]) * (HEAD_DIM ** -0.5)).astype(BF16)
    k_ref[...] = rope(qkv[:, C_K:C_V])
    v_ref[...] = qkv[:, C_V:C_XR]
    g_ref[...] = _sigmoid(_dot(zn, win_ref[:, C_GL:D_IN]) + bgate_ref[...]).astype(BF16)
    xr_s[slot_proj] = _dot(zn, win_ref[:, C_XR:C_GR])
    gr_s[slot_proj] = _dot(zn, win_ref[:, C_GR:C_GL])


def _inproj_call(x2d, rope_tabs, iconv, ih, wts, *, tm, seg, tiles_per_stream, rope_tiles, name):
    n = x2d.shape[0]
    nt = n // tm
    nseg = tm // seg
    n_streams = n // (seg * tiles_per_stream) if tiles_per_stream > 1 else n // seg
    shared_init = iconv.shape[0] == 1

    def cur(i):
        return jnp.minimum(i, nt - 1)

    def prev(i):
        return jnp.maximum(i - 1, 0)

    def full(a):
        return pl.BlockSpec(a.shape, lambda i: (0,) * a.ndim)

    def rows(w):
        return pl.BlockSpec((tm, w), lambda i: (cur(i), 0))

    rope_spec = pl.BlockSpec((tm, LANE), lambda i: (cur(i) % rope_tiles, 0))
    if shared_init:
        init_spec = pl.BlockSpec((1, SUBLANE, D_RNN), lambda i: (0, 0, 0))
    else:
        init_spec = pl.BlockSpec((nseg, SUBLANE, D_RNN), lambda i: (prev(i), 0, 0))
    state_spec = pl.BlockSpec((nseg, SUBLANE, D_RNN),
                              lambda i: (prev(i) // tiles_per_stream, 0, 0))

    (lng, lnb, win, bgate, cw, cb, wrg, ba, bx, lam) = wts
    kern = functools.partial(_inproj_kernel, seg=seg, nseg=nseg, tiles_per_stream=tiles_per_stream)
    return pl.pallas_call(
        kern,
        grid=(nt + 1,),
        in_specs=[rows(D_MODEL), full(lng), full(lnb), full(win), full(bgate),
                  rope_spec, rope_spec, rope_spec,
                  full(cw), full(cb), full(wrg), full(ba), full(bx), full(lam),
                  init_spec, init_spec],
        out_specs=[rows(D_ATTN), rows(D_KV), rows(D_KV),
                   pl.BlockSpec((tm, D_RNN), lambda i: (prev(i), 0)),
                   rows(2 * D_MODEL), state_spec, state_spec],
        out_shape=[jax.ShapeDtypeStruct((n, D_ATTN), BF16),
                   jax.ShapeDtypeStruct((n, D_KV), F32),
                   jax.ShapeDtypeStruct((n, D_KV), F32),
                   jax.ShapeDtypeStruct((n, D_RNN), BF16),
                   jax.ShapeDtypeStruct((n, 2 * D_MODEL), BF16),
                   jax.ShapeDtypeStruct((n_streams, SUBLANE, D_RNN), F32),
                   jax.ShapeDtypeStruct((n_streams, SUBLANE, D_RNN), F32)],
        scratch_shapes=[pltpu.VMEM((2, tm, D_RNN), F32),
                        pltpu.VMEM((2, tm, D_RNN), F32),
                        pltpu.VMEM((SUBLANE + seg, D_RNN), F32),
                        pltpu.VMEM((seg, D_RNN), F32),
                        pltpu.VMEM((seg, D_RNN), F32),
                        pltpu.VMEM((SUBLANE, D_RNN), F32)],
        compiler_params=_params(),
        name=name,
    )(x2d, lng, lnb, win, bgate, *rope_tabs, cw, cb, wrg, ba, bx, lam, iconv, ih)


HIST = (CONV_W - 1) * SUBLANE
N_SLAB = D_RNN // LANE


def _inproj_tm_kernel(x_ref, lng_ref, lnb_ref, win_ref, bgate_ref, rc_ref, rs1_ref, rs2_ref,
                      cw_ref, cb_ref, wrg_ref, ba_ref, bx_ref, lam_ref, ihist_ref, ih_ref,
                      *rest, tt, n_side):
    side_in = rest[:n_side]
    q_ref, k_ref, v_ref, rnn_ref, g_ref, hist_ref, h_ref = rest[n_side:n_side + 7]
    side_out = rest[n_side + 7:2 * n_side + 7]
    xt_s, gr_s, abuf, ubuf, ibuf, hist, hcar = rest[2 * n_side + 7:]
    for src, dst in zip(side_in, side_out):
        dst[...] = src[...].astype(BF16)
    nb = SUBLANE
    rows = nb * tt
    i = pl.program_id(0)
    slot_proj = i % 2
    slot_rec = 1 - slot_proj

    @pl.when(i == 0)
    def _():
        xt_s[...] = jnp.zeros_like(xt_s)
        gr_s[...] = jnp.zeros_like(gr_s)
        hist[...] = jnp.zeros_like(hist)
        hcar[...] = jnp.zeros_like(hcar)

    @pl.when(i == 1)
    def _():
        hist[...] = ihist_ref[...]
        hcar[...] = ih_ref[...]

    c8 = LRU_C * jax.nn.softplus(-lam_ref[...])
    cw = cw_ref[...]
    for j in range(N_SLAB):
        sl = slice(j * LANE, (j + 1) * LANE)
        xt = xt_s.at[slot_rec, j]
        xt[0:HIST, :] = hist[j]
        xc = (xt[HIST:HIST + rows, :] * cw[3:4, sl]
              + xt[HIST - nb:HIST - nb + rows, :] * cw[2:3, sl]
              + xt[HIST - 2 * nb:HIST - 2 * nb + rows, :] * cw[1:2, sl]
              + xt[0:rows, :] * cw[0:1, sl]
              + cb_ref[:, sl])
        hist[j] = xt[rows:rows + HIST, :]
        ri = _dot(xc.astype(BF16), wrg_ref[j])
        ubuf[j] = xc
        abuf[j] = ri[:, :RNN_BLOCK]
        ibuf[j] = ri[:, RNN_BLOCK:]
    gates_done = jnp.minimum(jnp.abs(ri[0:1, 0:RNN_BLOCK]), 0.0)
    for j in range(N_SLAB):
        sl = slice(j * LANE, (j + 1) * LANE)
        xc = ubuf[j]
        r = _sigmoid(abuf[j] + (ba_ref[:, sl] + gates_done))
        ig = _sigmoid(ibuf[j] + (bx_ref[:, sl] + gates_done))
        log_a = -c8[:, sl] * r
        a = jnp.exp(log_a)
        y = -jnp.tanh(log_a) * (a * a + 1.0)
        mult = jnp.where(y > 0.0, y * lax.rsqrt(y), 0.0)
        abuf[j] = a
        ubuf[j] = mult * ig * xc

    for j in range(N_SLAB):
        h = hcar[j]
        for t in range(tt):
            h = abuf[j, t * nb:(t + 1) * nb, :] * h + ubuf[j, t * nb:(t + 1) * nb, :]
            ubuf[j, t * nb:(t + 1) * nb, :] = h
        hcar[j] = h
    h_ref[...] = hcar[...]
    hist_ref[...] = hist[...]

    for b in range(nb):
        for j in range(N_SLAB):
            sl = slice(j * LANE, (j + 1) * LANE)
            hb = ubuf.at[j][pl.ds(b, tt, stride=nb), :]
            gate = jax.nn.gelu(gr_s[slot_rec, b * tt:(b + 1) * tt, sl])
            rnn_ref[b, :, sl] = (hb * gate).astype(BF16)

    zn = _layer_norm(x_ref[...].reshape(rows, D_MODEL), lng_ref[...], lnb_ref[...]).astype(BF16)
    qkv = _dot(zn, win_ref[:, C_Q:C_XR])
    rc, rs1, rs2 = (jnp.concatenate([t_ref[...]] * nb, axis=0) for t_ref in (rc_ref, rs1_ref, rs2_ref))

    def rope(t):
        return t * rc + pltpu.roll(t, 8, 1) * rs1 + pltpu.roll(t, LANE - 8, 1) * rs2

    qs = [(rope(qkv[:, j * LANE:(j + 1) * LANE]) * (HEAD_DIM ** -0.5)).astype(BF16)
          for j in range(D_ATTN // LANE)]
    kk = rope(qkv[:, C_K:C_V])
    g = _sigmoid(_dot(zn, win_ref[:, C_GL:D_IN]) + bgate_ref[...]).astype(BF16)
    xr = _dot(zn, win_ref[:, C_XR:C_GR])
    gr_s[slot_proj] = _dot(zn, win_ref[:, C_GR:C_GL])
    for b in range(nb):
        rb = slice(b * tt, (b + 1) * tt)
        for j, qj in enumerate(qs):
            q_ref[b, :, j * LANE:(j + 1) * LANE] = qj[rb, :]
        k_ref[b] = kk[rb, :]
        v_ref[b] = qkv[rb, C_V:C_XR]
        g_ref[b] = g[rb, :]
        for j in range(N_SLAB):
            xt_s.at[slot_proj, j][pl.ds(HIST + b, tt, stride=nb), :] = xr[rb, j * LANE:(j + 1) * LANE]


def _inproj_tm_call(x3d, rope_tabs, ihist, ih, wts, expert_w, *, tt, name):
    nb, s, _ = x3d.shape
    assert nb == SUBLANE and s % tt == 0 and tt % (2 * SUBLANE) == 0
    nt = s // tt
    rows = nb * tt
    assert all(w.shape[0] % (nt * 2 * SUBLANE) == 0 for w in expert_w)

    def cur(i):
        return jnp.minimum(i, nt - 1)

    def prev(i):
        return jnp.maximum(i - 1, 0)

    def full(a):
        return pl.BlockSpec(a.shape, lambda i: (0,) * a.ndim)

    def tile(w, which=cur):
        return pl.BlockSpec((nb, tt, w), lambda i: (0, which(i), 0))

    rope_spec = pl.BlockSpec((tt, LANE), lambda i: (cur(i), 0))
    ew_specs = [pl.BlockSpec((w.shape[0] // nt, w.shape[1]), lambda i: (cur(i), 0)) for w in expert_w]
    (lng, lnb, win, bgate, cw, cb, wrg, ba, bx, lam) = wts
    return pl.pallas_call(
        functools.partial(_inproj_tm_kernel, tt=tt, n_side=len(expert_w)),
        grid=(nt + 1,),
        in_specs=[tile(D_MODEL), full(lng), full(lnb), full(win), full(bgate),
                  rope_spec, rope_spec, rope_spec,
                  full(cw), full(cb), full(wrg), full(ba), full(bx), full(lam),
                  full(ihist), full(ih)] + ew_specs,
        out_specs=[tile(D_ATTN), tile(D_KV), tile(D_KV), tile(D_RNN, prev), tile(2 * D_MODEL),
                   full(ihist), full(ih)] + ew_specs,
        out_shape=[jax.ShapeDtypeStruct((nb, s, D_ATTN), BF16),
                   jax.ShapeDtypeStruct((nb, s, D_KV), F32),
                   jax.ShapeDtypeStruct((nb, s, D_KV), F32),
                   jax.ShapeDtypeStruct((nb, s, D_RNN), BF16),
                   jax.ShapeDtypeStruct((nb, s, 2 * D_MODEL), BF16),
                   jax.ShapeDtypeStruct(ihist.shape, F32),
                   jax.ShapeDtypeStruct(ih.shape, F32)]
                  + [jax.ShapeDtypeStruct(w.shape, BF16) for w in expert_w],
        scratch_shapes=[pltpu.VMEM((2, N_SLAB, HIST + rows, LANE), F32),
                        pltpu.VMEM((2, rows, D_RNN), F32),
                        pltpu.VMEM((N_SLAB, rows, LANE), F32),
                        pltpu.VMEM((N_SLAB, rows, LANE), F32),
                        pltpu.VMEM((N_SLAB, rows, LANE), F32),
                        pltpu.VMEM((N_SLAB, HIST, LANE), F32),
                        pltpu.VMEM((N_SLAB, SUBLANE, LANE), F32)],
        compiler_params=_params(),
        name=name,
    )(x3d, lng, lnb, win, bgate, *rope_tabs, cw, cb, wrg, ba, bx, lam, ihist, ih, *expert_w)


N_KEYS = N_META + WINDOW + CHUNK
KEY_PAD = 2 * LANE


def _attn_kernel(sink_ref, q_ref, km_ref, vm_ref, kp_ref, vp_ref, ko_ref, vo_ref, o_ref,
                 *, nchunk, tiles_per_stream, mask_first, separate):
    i = pl.program_id(0)
    first = (i % tiles_per_stream) == 0
    lane = lax.broadcasted_iota(jnp.int32, (1, LANE), 1)
    lo = lane < HEAD_DIM

    def slabs(x):
        xs = pltpu.roll(x, HEAD_DIM, 1)
        return (jnp.where(lo, x, xs).astype(BF16), jnp.where(lo, xs, x).astype(BF16))

    km, vm = slabs(km_ref[...]), slabs(vm_ref[...])
    kp, vp = slabs(kp_ref[...]), slabs(vp_ref[...])
    ko, vo = slabs(ko_ref[...]), slabs(vo_ref[...])
    zpad = jnp.zeros((KEY_PAD - N_KEYS, LANE), BF16)
    kpos = lax.broadcasted_iota(jnp.int32, (1, KEY_PAD), 1)
    qrow = lax.broadcasted_iota(jnp.int32, (4 * CHUNK, 1), 0)
    q_per_kv = N_HEADS // N_KV

    def keys(meta, prev, own, c):
        if separate:
            return [meta[N_META * c:N_META * (c + 1)], prev[WINDOW * c:WINDOW * (c + 1)],
                    own[CHUNK * c:CHUNK * (c + 1)]]
        parts = [meta]
        if CHUNK * c < WINDOW:
            parts.append(prev[CHUNK * c:WINDOW])
        parts.append(own[max(CHUNK * c - WINDOW, 0):CHUNK * (c + 1)])
        return parts

    for c in range(nchunk):
        qc = q_ref[c * CHUNK:(c + 1) * CHUNK, :]
        for kv in range(N_KV):
            kc = jnp.concatenate(keys(km[kv], kp[kv], ko[kv], c) + [zpad], axis=0)
            vc = jnp.concatenate(keys(vm[kv], vp[kv], vo[kv], c) + [zpad], axis=0)
            qs = []
            for g in range(q_per_kv):
                h = kv * q_per_kv + g
                slab = qc[:, (h // 2) * LANE:(h // 2 + 1) * LANE]
                keep = lo if h % 2 == 0 else jnp.logical_not(lo)
                qs.append(jnp.where(keep, slab, jnp.zeros_like(slab)))
            qst = jnp.concatenate(qs, axis=0)
            s = lax.dot_general(qst, kc, (((1,), (1,)), ((), ())), preferred_element_type=F32)
            bad = kpos >= N_KEYS
            n_masked = WINDOW - CHUNK * c
            if mask_first and n_masked > 0:
                lim = N_META + jnp.where(first, n_masked, 0)
                bad = bad | ((kpos >= N_META) & (kpos < lim))
            s = jnp.where(bad, -jnp.inf, s)
            sink = jnp.full((4 * CHUNK, 1), sink_ref[kv * q_per_kv + q_per_kv - 1], F32)
            for g in range(q_per_kv - 2, -1, -1):
                sink = jnp.where(qrow < (g + 1) * CHUNK, sink_ref[kv * q_per_kv + g], sink)
            m = jnp.maximum(jnp.max(s, axis=-1, keepdims=True), sink)
            p = jnp.exp(s - m)
            denom = jnp.sum(p, axis=-1, keepdims=True) + jnp.exp(sink - m)
            o = _dot(p.astype(BF16), vc) / denom
            for jj in range(q_per_kv // 2):
                g0 = 2 * jj
                oslab = jnp.where(lo, o[g0 * CHUNK:(g0 + 1) * CHUNK, :],
                                  o[(g0 + 1) * CHUNK:(g0 + 2) * CHUNK, :])
                j = kv * (q_per_kv // 2) + jj
                o_ref[c * CHUNK:(c + 1) * CHUNK, j * LANE:(j + 1) * LANE] = oslab.astype(BF16)


def _attn_call(sinks, q, kmeta, vmeta, kprev, vprev, kown, vown, *, tq, tiles_per_stream,
               mask_first, prev_map, meta_map, name, separate=False):
    n = q.shape[0]
    nt = n // tq
    nchunk = tq // CHUNK
    kern = functools.partial(_attn_kernel, nchunk=nchunk, tiles_per_stream=tiles_per_stream,
                             mask_first=mask_first, separate=separate)
    per_tile = nchunk if separate else 1
    meta_spec = pl.BlockSpec((per_tile * N_META, D_KV), meta_map)
    prev_spec = pl.BlockSpec((per_tile * WINDOW, D_KV), prev_map)
    own_spec = pl.BlockSpec((tq, D_KV), lambda i: (i, 0))
    return pl.pallas_call(
        kern,
        grid=(nt,),
        in_specs=[pl.BlockSpec(memory_space=pltpu.SMEM),
                  pl.BlockSpec((tq, D_ATTN), lambda i: (i, 0)),
                  meta_spec, meta_spec, prev_spec, prev_spec, own_spec, own_spec],
        out_specs=pl.BlockSpec((tq, D_ATTN), lambda i: (i, 0)),
        out_shape=jax.ShapeDtypeStruct((n, D_ATTN), BF16),
        compiler_params=_params(),
        name=name,
    )(sinks, q, kmeta, vmeta, kprev, vprev, kown, vown)


def _merge_kernel(o_ref, rnn_ref, g_ref, x_ref, lng_ref, lnb_ref, wpa_ref, wpr_ref, wo_ref,
                  l1g_ref, l1b_ref, wr_ref, br_ref, triu_ref, cin_ref,
                  z1_ref, z1t_ref, slot_ref, cout_ref, cnt):
    i = pl.program_id(0)

    @pl.when(i == 0)
    def _():
        cnt[...] = jnp.concatenate([cin_ref[...]] * (cnt.shape[1] // LANE), axis=1)

    g = g_ref[...]
    pa = _dot(o_ref[...], wpa_ref[...])
    pr = _dot(rnn_ref[...], wpr_ref[...])
    mixed = g[:, :D_MODEL].astype(F32) * pa + g[:, D_MODEL:].astype(F32) * pr
    mix = _dot(mixed.astype(BF16), wo_ref[...])
    zn = _layer_norm(x_ref[...], lng_ref[...], lnb_ref[...])
    z1 = _layer_norm(ALPHA * zn + mix, l1g_ref[...], l1b_ref[...])
    z1_ref[...] = z1

    z_hi = z1.astype(BF16)
    z_lo = (z1 - z_hi.astype(F32)).astype(BF16)
    la = _dot(z_hi, wr_ref[...])
    lb = _dot(z_lo, wr_ref[...])
    logit = la[:, :LANE] + la[:, LANE:] + lb[:, :LANE] + lb[:, LANE:] + br_ref[...]

    tm = logit.shape[0]
    lt = logit.T[0:ROUTE_ROWS, :]
    row = lax.broadcasted_iota(jnp.int32, (ROUTE_ROWS, tm), 0)
    rin = (row & 3).astype(F32)
    seg = (row >> 2).astype(F32)
    odd1 = (row & 1) == 1
    odd2 = (row & 2) == 2
    ninf = -jnp.inf

    def seg_reduce(x, op):
        y = op(x, jnp.where(odd1, pltpu.roll(x, 1, 0), pltpu.roll(x, ROUTE_ROWS - 1, 0)))
        return op(y, jnp.where(odd2, pltpu.roll(y, 2, 0), pltpu.roll(y, ROUTE_ROWS - 2, 0)))

    xl = jnp.where(row < N_GROUPS * (1 + EXPERTS_PER_GROUP), lt, ninf)
    m1 = seg_reduce(xl, jnp.maximum)
    i1 = seg_reduce(jnp.where(xl == m1, rin, 4.0), jnp.minimum)
    xl2 = jnp.where(rin == i1, ninf, xl)
    m2 = seg_reduce(xl2, jnp.maximum)
    i2 = seg_reduce(jnp.where(xl2 == m2, rin, 4.0), jnp.minimum)
    esum = seg_reduce(jnp.where(row < N_GROUPS, jnp.exp(xl - m1), 0.0), jnp.add)
    gidx = i1[0:1, :]
    gw = 1.0 / esum[0:1, :]
    chosen = seg == gidx + 1.0

    def pick(x):
        return jnp.max(jnp.where(chosen, x, ninf), axis=0, keepdims=True)

    v1, v2, l1, l2 = pick(m1), pick(m2), pick(i1), pick(i2)
    e21 = jnp.exp(v2 - v1)
    w1 = gw / (1.0 + e21)
    w2 = gw * e21 / (1.0 + e21)
    first_lo = l1 < l2
    ea, eb = jnp.minimum(l1, l2), jnp.maximum(l1, l2)
    w_lo = jnp.where(first_lo, w1, w2)
    w_hi = jnp.where(first_lo, w2, w1)
    lex = ea * (7.0 - ea) * 0.5 + (eb - ea - 1.0)
    pair = jnp.where(lex == 3.0, 4.0, jnp.where(lex == 4.0, 3.0, lex))
    flipped = lex == 5.0
    wa = jnp.where(flipped, w_hi, w_lo)
    wb = jnp.where(flipped, w_lo, w_hi)
    cls = gidx * N_PAIR + pair

    onehot = row.astype(F32) == cls
    ohf = jnp.where(onehot, 1.0, 0.0)
    before = _dot(ohf.astype(BF16), triu_ref[...])
    rank = jnp.sum(jnp.where(onehot, before + cnt[...], 0.0), axis=0, keepdims=True)
    cnt[...] = cnt[...] + jnp.sum(ohf, axis=1, keepdims=True)
    cout_ref[...] = cnt[:, 0:LANE]
    slot_ref[0] = jnp.broadcast_to(cls, (SUBLANE, tm))
    slot_ref[1] = jnp.broadcast_to(rank, (SUBLANE, tm))

    srow = lax.broadcasted_iota(jnp.int32, (LANE, tm), 0)
    scalars = jnp.where(srow == 1, wa, jnp.where(srow == 2, wb, 0.0)).T
    _store_token_rows(z1t_ref, z1, scalars)


def _merge_call(o, rnn, g, x2d, counts_in, wts, *, name):
    n = x2d.shape[0]
    nt = n // TM

    def full(a):
        return pl.BlockSpec(a.shape, lambda i: (0,) * a.ndim)

    def rows(w):
        return pl.BlockSpec((TM, w), lambda i: (i, 0))

    return pl.pallas_call(
        _merge_kernel,
        grid=(nt,),
        in_specs=[rows(D_ATTN), rows(D_RNN), rows(2 * D_MODEL), rows(D_MODEL)]
                 + [full(w) for w in wts] + [full(counts_in)],
        out_specs=[rows(D_MODEL), pl.BlockSpec((TM * PITCH, LANE), lambda i: (i, 0)),
                   pl.BlockSpec((None, 2, SUBLANE, TM), lambda i: (i, 0, 0, 0)),
                   pl.BlockSpec((ROUTE_ROWS, LANE), lambda i: (0, 0))],
        out_shape=[jax.ShapeDtypeStruct((n, D_MODEL), F32),
                   jax.ShapeDtypeStruct((n * PITCH, LANE), F32),
                   jax.ShapeDtypeStruct((nt, 2, SUBLANE, TM), F32),
                   jax.ShapeDtypeStruct((ROUTE_ROWS, LANE), F32)],
        scratch_shapes=[pltpu.VMEM((ROUTE_ROWS, TM), F32)],
        compiler_params=_params(),
        name=name,
    )(o, rnn, g, x2d, *wts, counts_in)


def _plan_kernel(cnt_ref, slot_p_ref, slot_s_ref, pos_p_ref, pos_s_ref, tiles_ref):
    shift = TS.bit_length() - 1
    off, end, fill, fill_p = [], [], [], []
    run = jnp.int32(0)
    last = jnp.int32(0)
    for c in range(N_CLASS):
        n = cnt_ref[c]
        off.append(run)
        fill.append(run + n)
        fill_p.append(run + cnt_ref[N_CLASS + c])
        run = run + lax.shift_left(lax.shift_right_logical(n + (TS - 1), shift), shift)
        end.append(run)
        last = jnp.where(n > 0, c, last)

    def slots(slot_ref, pos_ref):
        cls = slot_ref[:, 0]
        o = jnp.zeros(cls.shape, jnp.int32)
        for c in range(N_CLASS):
            o = jnp.where(cls == float(c), off[c], o)
        pos_ref[...] = ((o + slot_ref[:, 1].astype(jnp.int32)) * PITCH)[:, 0:1, :]

    slots(slot_p_ref, pos_p_ref)
    slots(slot_s_ref, pos_s_ref)

    tstart = lax.broadcasted_iota(jnp.int32, (SUBLANE, LANE), 1) * TS
    tcls = jnp.zeros((SUBLANE, LANE), jnp.int32)
    for c in range(N_CLASS):
        tcls = tcls + jnp.where(tstart >= end[c], 1, 0)
    tcls = jnp.minimum(tcls, last)
    zero = jnp.zeros((SUBLANE, LANE), jnp.int32)
    grp, ea, eb, rows, rows_p = zero, zero, zero, zero, zero
    for c in range(N_CLASS):
        sel = tcls == c
        first = (c // N_PAIR) * EXPERTS_PER_GROUP
        grp = jnp.where(sel, c // N_PAIR, grp)
        ea = jnp.where(sel, first + int(PAIR_A[c % N_PAIR]), ea)
        eb = jnp.where(sel, first + int(PAIR_B[c % N_PAIR]), eb)
        rows = jnp.where(sel, fill[c], rows)
        rows_p = jnp.where(sel, fill_p[c], rows_p)
    rows = jnp.clip(rows - tstart, 0, TS)
    rows_p = jnp.clip(rows_p - tstart, 0, TS)
    sub = lax.broadcasted_iota(jnp.int32, (SUBLANE, LANE), 0)
    tiles_ref[...] = jnp.where(sub == 0, grp, jnp.where(sub == 1, ea, jnp.where(
        sub == 2, eb, jnp.where(sub == 3, rows, rows_p))))


def _plan_call(counts, slot_p, slot_s):
    def full(a):
        return pl.BlockSpec(a.shape, lambda i: (0,) * a.ndim)

    pos_shapes = [jax.ShapeDtypeStruct((a.shape[0], 1, TM), jnp.int32) for a in (slot_p, slot_s)]
    tiles_shape = jax.ShapeDtypeStruct((SUBLANE, LANE), jnp.int32)
    return pl.pallas_call(
        _plan_kernel,
        grid=(1,),
        in_specs=[pl.BlockSpec(memory_space=pltpu.SMEM), full(slot_p), full(slot_s)],
        out_specs=[full(s) for s in pos_shapes] + [full(tiles_shape)],
        out_shape=pos_shapes + [tiles_shape],
        compiler_params=_params(),
        name="slot_plan",
    )(counts, slot_p, slot_s)


def _dispatch_kernel(pos_ref, extra_ref, z1t_ref, xs_ref, rowbuf, sem, *, nt, fresh):
    i = pl.program_id(0)
    slot = i % 2
    tile_rows = TS * PITCH

    def row_copy(sl, r, p):
        return pltpu.make_async_copy(rowbuf.at[sl, pl.ds(r * PITCH, PITCH), :],
                                     xs_ref.at[pl.ds(p, PITCH), :], sem.at[sl])

    def drain(sl):
        pltpu.make_async_copy(rowbuf.at[sl], xs_ref.at[pl.ds(0, TD * PITCH), :], sem.at[sl]).wait()

    if fresh:
        @pl.when(i == 0)
        def _():
            rowbuf[1, 0:tile_rows, :] = jnp.zeros((tile_rows, LANE), F32)
            n_tiles = extra_ref.shape[0]

            def fill_copy(t):
                dst = pl.multiple_of(t * tile_rows, tile_rows)
                return pltpu.make_async_copy(rowbuf.at[1, pl.ds(0, tile_rows), :],
                                             xs_ref.at[pl.ds(dst, tile_rows), :], sem.at[2])

            def fill(t, c):
                @pl.when(extra_ref[t] < TS)
                def _():
                    fill_copy(t).start()
                return c
            lax.fori_loop(0, n_tiles, fill, 0)

            def fill_wait(t, c):
                @pl.when(extra_ref[t] < TS)
                def _():
                    fill_copy(0).wait()
                return c
            lax.fori_loop(0, n_tiles, fill_wait, 0)

    @pl.when(i >= 2)
    def _():
        drain(slot)

    rowbuf[slot] = z1t_ref[...]

    def issue(rb, c):
        for u in range(DMA_UNROLL):
            r = rb * DMA_UNROLL + u
            row_copy(slot, r, pos_ref[0, r]).start(priority=u % 2)
        return c
    lax.fori_loop(0, TD // DMA_UNROLL, issue, 0)

    @pl.when(i == nt - 1)
    def _():
        drain(slot)
        if nt > 1:
            drain(1 - slot)


def _dispatch_call(pos3, z1t, xs, tile_rows, *, name):
    nt = z1t.shape[0] // (TD * PITCH)
    fresh = xs is None
    n_slots = tile_rows.shape[0] * TS if fresh else xs.shape[0] // PITCH
    extra = tile_rows if fresh else xs
    extra_spec = pl.BlockSpec(memory_space=pltpu.SMEM if fresh else pl.ANY)
    return pl.pallas_call(
        functools.partial(_dispatch_kernel, nt=nt, fresh=fresh),
        grid=(nt,),
        in_specs=[pl.BlockSpec((None, 1, TD), lambda i: (i, 0, 0), memory_space=pltpu.SMEM),
                  extra_spec,
                  pl.BlockSpec((TD * PITCH, LANE), lambda i: (i, 0))],
        out_specs=pl.BlockSpec(memory_space=pl.ANY),
        out_shape=jax.ShapeDtypeStruct((n_slots * PITCH, LANE), F32),
        input_output_aliases={} if fresh else {1: 0},
        scratch_shapes=[pltpu.VMEM((2, TD * PITCH, LANE), F32), pltpu.SemaphoreType.DMA((3,))],
        compiler_params=_params(),
        name=name,
    )(pos3, extra, z1t)


def _moe_kernel(grp_ref, ea_ref, eb_ref, val_ref, xs_ref, wg_ref, wu_ref, wd_ref, ys_ref):
    del grp_ref
    i = pl.program_id(0)

    @pl.when(val_ref[i] > 0)
    def _():
        xb = _load_token_rows(xs_ref, TS).astype(BF16)
        scalars = xs_ref[pl.ds(ROW_TILE, TS, stride=PITCH), :]
        ea = ea_ref[i] % EXPERTS_PER_GROUP
        eb = eb_ref[i] % EXPERTS_PER_GROUP
        ga, ua = _dot(xb, wg_ref[ea]), _dot(xb, wu_ref[ea])
        gb, ub = _dot(xb, wg_ref[eb]), _dot(xb, wu_ref[eb])
        ya = _dot((jax.nn.silu(ga) * ua).astype(BF16), wd_ref[ea])
        yb = _dot((jax.nn.silu(gb) * ub).astype(BF16), wd_ref[eb])
        y = scalars[:, 1:2] * ya + scalars[:, 2:3] * yb
        _store_token_rows(ys_ref, y, jnp.zeros((TS, LANE), F32))

    @pl.when(val_ref[i] == 0)
    def _():
        ys_ref[...] = jnp.zeros_like(ys_ref)


def _moe_call(tile_grp, tile_ea, tile_eb, tile_valid, xs, wg, wu, wd):
    n_tiles = xs.shape[0] // (TS * PITCH)
    wspec_in = pl.BlockSpec((None, EXPERTS_PER_GROUP, D_MODEL, D_EXPERT),
                            lambda i, grp, *_: (grp[i], 0, 0, 0))
    wspec_out = pl.BlockSpec((None, EXPERTS_PER_GROUP, D_EXPERT, D_MODEL),
                             lambda i, grp, *_: (grp[i], 0, 0, 0))
    grid_spec = pltpu.PrefetchScalarGridSpec(
        num_scalar_prefetch=4,
        grid=(n_tiles,),
        in_specs=[pl.BlockSpec((TS * PITCH, LANE), lambda i, *_: (i, 0)),
                  wspec_in, wspec_in, wspec_out],
        out_specs=pl.BlockSpec((TS * PITCH, LANE), lambda i, *_: (i, 0)),
    )
    return pl.pallas_call(
        _moe_kernel,
        grid_spec=grid_spec,
        out_shape=jax.ShapeDtypeStruct((n_tiles * TS * PITCH, LANE), F32),
        compiler_params=_params(),
        name="moe_experts",
    )(tile_grp, tile_ea, tile_eb, tile_valid, xs, wg, wu, wd)


def _combine_kernel(pos_ref, posn_ref, z1_ref, l2g_ref, l2b_ref, ys_ref, out_ref, ybuf, sem, *, nt):
    i = pl.program_id(0)
    slot = i % 2

    def row_copy(sl, r, p):
        return pltpu.make_async_copy(ys_ref.at[pl.ds(p, ROW_TILE), :],
                                     ybuf.at[sl, pl.ds(r * PITCH, ROW_TILE), :], sem.at[sl])

    def issue(sl, pref):
        def body(rb, c):
            for u in range(DMA_UNROLL):
                r = rb * DMA_UNROLL + u
                row_copy(sl, r, pref[0, r]).start(priority=u % 2)
            return c
        lax.fori_loop(0, TM // DMA_UNROLL, body, 0)

    @pl.when(i == 0)
    def _():
        issue(0, pos_ref)

    @pl.when(i + 1 < nt)
    def _():
        issue(1 - slot, posn_ref)

    pltpu.make_async_copy(ys_ref.at[pl.ds(0, TM * ROW_TILE), :],
                          ybuf.at[slot, pl.ds(0, TM * ROW_TILE), :], sem.at[slot]).wait()

    moe = _load_token_rows(ybuf.at[slot], TM)
    out_ref[...] = _layer_norm(ALPHA * z1_ref[...] + moe, l2g_ref[...], l2b_ref[...])


def _combine_call(pos3, z1, l2g, l2b, ys, *, name):
    n = z1.shape[0]
    nt = n // TM
    return pl.pallas_call(
        functools.partial(_combine_kernel, nt=nt),
        grid=(nt,),
        in_specs=[pl.BlockSpec((None, 1, TM), lambda i: (i, 0, 0), memory_space=pltpu.SMEM),
                  pl.BlockSpec((None, 1, TM), lambda i: (jnp.minimum(i + 1, nt - 1), 0, 0),
                               memory_space=pltpu.SMEM),
                  pl.BlockSpec((TM, D_MODEL), lambda i: (i, 0)),
                  pl.BlockSpec(l2g.shape, lambda i: (0, 0)),
                  pl.BlockSpec(l2b.shape, lambda i: (0, 0)),
                  pl.BlockSpec(memory_space=pl.ANY)],
        out_specs=pl.BlockSpec((TM, D_MODEL), lambda i: (i, 0)),
        out_shape=jax.ShapeDtypeStruct((n, D_MODEL), F32),
        scratch_shapes=[pltpu.VMEM((2, TM * PITCH, LANE), F32), pltpu.SemaphoreType.DMA((2,))],
        compiler_params=_params(),
        name=name,
    )(pos3, pos3, z1, l2g, l2b, ys)


def _rope_tables(pos, reps=1):
    half = ROPE_DIM // 2
    inv_freq = ROPE_THETA ** (-np.arange(half, dtype=np.float64) / half)
    ang = pos.astype(np.float64)[:, None] * inv_freq[None, :]
    cos, sin = np.cos(ang), np.sin(ang)
    p = pos.shape[0]
    ones = np.ones((p, HEAD_DIM - ROPE_DIM))
    zeros = np.zeros((p, HEAD_DIM - ROPE_DIM))
    zh = np.zeros((p, half))
    c = np.concatenate([cos, cos, ones], axis=1)
    s1 = np.concatenate([zh, sin, zeros], axis=1)
    s2 = np.concatenate([-sin, zh, zeros], axis=1)
    rep = LANE // HEAD_DIM
    return tuple(jnp.asarray(np.tile(t, (reps, rep)).astype(np.float32)) for t in (c, s1, s2))


def kernel(x_prompt, x_sample, cache_meta_k, cache_meta_v, cache_win_k, cache_win_v, state_conv, state_h, meta_tokens, ln_in_g, ln_in_b, w_in, b_gate, attn_sinks, conv_w, conv_b, w_rg_a, b_rg_a, w_rg_x, b_rg_x, lru_lambda, w_branch_attn, w_branch_rnn, w_out, ln1_g, ln1_b, w_group, b_group, w_router, b_router, w_gate, w_up, w_down, ln2_g, ln2_b):
    B, S, _ = x_prompt.shape
    DB, TSQ, _ = x_sample.shape
    assert DEPTH == 1 and w_in.shape == (DEPTH, D_MODEL, D_IN)
    assert S % TM == 0 and (DB * TSQ) % TM == 0 and TM % TSQ == 0 and TSQ == CHUNK
    assert cache_win_k.shape[2] == WINDOW and S >= WINDOW
    l = 0

    def row(v):
        return v.reshape(1, -1).astype(F32)

    lng, lnb = row(ln_in_g), row(ln_in_b)
    win = w_in[l].astype(BF16)
    cw = jnp.concatenate([conv_w[l], jnp.zeros((SUBLANE - CONV_W, D_RNN), F32)], axis=0)
    wrg = jnp.concatenate([w_rg_a[l], w_rg_x[l]], axis=-1).astype(BF16)
    in_wts = (lng, lnb, win, row(b_gate[l]), cw, row(conv_b[l]), wrg,
              row(b_rg_a[l]), row(b_rg_x[l]), row(lru_lambda[l]))
    wr = jnp.concatenate([w_group[l], w_router[l].reshape(D_MODEL, -1)], axis=1)
    wr = jnp.pad(wr, ((0, 0), (0, LANE - wr.shape[1])))
    wr_hi = wr.astype(BF16)
    wr_lo = (wr - wr_hi.astype(F32)).astype(BF16)
    wr_cat = jnp.concatenate([wr_hi, wr_lo], axis=1)
    br = jnp.pad(jnp.concatenate([b_group[l], b_router[l].reshape(-1)]),
                 (0, LANE - N_GROUPS * (1 + EXPERTS_PER_GROUP))).reshape(1, LANE)
    merge_small = (row(ln1_g[l]), row(ln1_b[l]), wr_cat, br,
                   jnp.asarray(np.triu(np.ones((TM, TM), np.float32), 1), BF16))
    sinks = attn_sinks[l].astype(F32)

    rope_meta = _rope_tables(np.arange(N_META))
    rope_p = _rope_tables(N_META + np.arange(S))
    rope_s = _rope_tables(N_META + PAST_LEN + np.arange(TSQ), reps=TM_SAMPLE // TSQ)

    xp = x_prompt.reshape(B * S, D_MODEL)
    xs_ = x_sample.reshape(DB * TSQ, D_MODEL)
    n_p, n_s = B * S, DB * TSQ
    zero_state = jnp.zeros((1, SUBLANE, D_RNN), F32)

    _, k_m, v_m, _, _, conv_m, h_m = _inproj_call(
        meta_tokens.astype(F32), rope_meta, zero_state, zero_state, in_wts,
        tm=N_META, seg=N_META, tiles_per_stream=1, rope_tiles=1, name="inproj_meta")
    n_hist = CONV_W - 1
    ihist = conv_m[0, SUBLANE - n_hist:, :].reshape(n_hist, N_SLAB, LANE).transpose(1, 0, 2)
    ihist = jnp.broadcast_to(ihist[:, :, None, :], (N_SLAB, n_hist, B, LANE)).reshape(N_SLAB, HIST, LANE)
    ih = jnp.broadcast_to(h_m[0, 0].reshape(N_SLAB, 1, LANE), (N_SLAB, B, LANE))
    expert_w = tuple(w[l].astype(F32).reshape(-1, w.shape[-1])
                     for w in (w_gate, w_up, w_down, w_branch_attn, w_branch_rnn, w_out))
    q_p, k_p, v_p, rnn_p, g_p, hist_p, hfin_p, wg, wu, wd, wpa, wpr, wo = (
        a.reshape(n_p, -1) if a.shape[:2] == (B, S) else a
        for a in _inproj_tm_call(x_prompt.astype(F32), rope_p, ihist, ih, in_wts, expert_w,
                                 tt=TM // B, name="inproj_prompt"))
    wg = wg.reshape(N_GROUPS, EXPERTS_PER_GROUP, D_MODEL, D_EXPERT)
    wu = wu.reshape(N_GROUPS, EXPERTS_PER_GROUP, D_MODEL, D_EXPERT)
    wd = wd.reshape(N_GROUPS, EXPERTS_PER_GROUP, D_EXPERT, D_MODEL)
    merge_wts = (lng, lnb, wpa, wpr, wo) + merge_small
    iconv_s =jnp.pad(state_conv[l].astype(F32), ((0, 0), (SUBLANE - (CONV_W - 1), 0), (0, 0)))
    ih_s = jnp.broadcast_to(state_h[l].astype(F32)[:, None, :], (DB, SUBLANE, D_RNN))
    q_s, k_s, v_s, rnn_s, g_s, conv_s, h_s = _inproj_call(
        xs_, rope_s, iconv_s, ih_s, in_wts,
        tm=TM_SAMPLE, seg=TSQ, tiles_per_stream=1, rope_tiles=1, name="inproj_sample")

    tq_p = 2 * TM
    assert S % tq_p == 0
    tps = S // tq_p
    win_blocks = tq_p // WINDOW
    o_p = _attn_call(
        sinks, q_p, k_m, v_m, k_p, v_p, k_p, v_p, tq=tq_p, tiles_per_stream=tps, mask_first=True,
        prev_map=lambda i: (jnp.maximum(i * win_blocks - 1, 0), 0),
        meta_map=lambda i: (0, 0), name="attn_prompt")
    cmk = cache_meta_k[l].reshape(DB * N_META, D_KV).astype(F32)
    cmv = cache_meta_v[l].reshape(DB * N_META, D_KV).astype(F32)
    cwk = cache_win_k[l].reshape(DB * WINDOW, D_KV).astype(F32)
    cwv = cache_win_v[l].reshape(DB * WINDOW, D_KV).astype(F32)
    o_s = _attn_call(
        sinks, q_s, cmk, cmv, cwk, cwv, k_s, v_s, tq=4 * TSQ, tiles_per_stream=1, mask_first=False,
        prev_map=lambda i: (i, 0), meta_map=lambda i: (i, 0), name="attn_sample", separate=True)

    zero_cnt = jnp.zeros((ROUTE_ROWS, LANE), F32)
    z1_p, z1t_p, slot_p, cnt_p = _merge_call(
        o_p, rnn_p, g_p, xp, zero_cnt, merge_wts, name="merge_prompt")
    z1_s, z1t_s, slot_s, cnt_all = _merge_call(
        o_s, rnn_s, g_s, xs_, cnt_p, merge_wts, name="merge_sample")

    n_tiles = (n_p + n_s) // TS + N_CLASS
    assert n_tiles <= LANE
    counts = jnp.concatenate([cnt_all[:N_CLASS, 0], cnt_p[:N_CLASS, 0]]).astype(jnp.int32)
    pos_p, pos_s, tiles = _plan_call(counts, slot_p, slot_s)
    tile_grp, tile_ea, tile_eb, tile_rows, tile_rows_p = (tiles[r, :n_tiles] for r in range(5))

    xs1 = _dispatch_call(pos_p.reshape(-1, 1, TD), z1t_p, None, tile_rows_p, name="dispatch_prompt")
    xs2 = _dispatch_call(pos_s.reshape(-1, 1, TD), z1t_s, xs1, tile_rows, name="dispatch_sample")
    ys = _moe_call(tile_grp, tile_ea, tile_eb, tile_rows, xs2, wg, wu, wd)
    l2g, l2b = row(ln2_g[l]), row(ln2_b[l])
    y_p = _combine_call(pos_p, z1_p, l2g, l2b, ys, name="combine_prompt")
    y_s = _combine_call(pos_s, z1_s, l2g, l2b, ys, name="combine_sample")

    def heads(a, lead):
        return a.reshape(*lead, N_KV, HEAD_DIM)[None]

    tail = slice(SUBLANE - (CONV_W - 1), SUBLANE)
    y_prompt = y_p.reshape(B, S, D_MODEL)
    y_sample = y_s.reshape(DB, TSQ, D_MODEL)
    meta_k_p = jnp.broadcast_to(heads(k_m, (1, N_META)), (1, B, N_META, N_KV, HEAD_DIM))
    meta_v_p = jnp.broadcast_to(heads(v_m, (1, N_META)), (1, B, N_META, N_KV, HEAD_DIM))
    win_k_p = heads(k_p.reshape(B, S, D_KV)[:, S - WINDOW:], (B, WINDOW))
    win_v_p = heads(v_p.reshape(B, S, D_KV)[:, S - WINDOW:], (B, WINDOW))
    conv_out_p = hist_p.reshape(N_SLAB, n_hist, B, LANE).transpose(2, 1, 0, 3).reshape(
        1, B, n_hist, D_RNN)
    h_out_p = hfin_p.transpose(1, 0, 2).reshape(1, B, D_RNN)
    new_k_s = heads(k_s, (DB, TSQ))
    new_v_s = heads(v_s, (DB, TSQ))
    conv_out_s = conv_s[:, tail, :][None]
    h_out_s = h_s[:, 0, :][None].astype(state_h.dtype)
    return (y_prompt, y_sample, meta_k_p, meta_v_p, win_k_p, win_v_p, conv_out_p, h_out_p,
            new_k_s, new_v_s, conv_out_s, h_out_s)
```

```python
import functools

import numpy as np
import jax
import jax.numpy as jnp
from jax import lax
from jax.experimental import pallas as pl
from jax.experimental.pallas import tpu as pltpu

F32 = jnp.float32
BF16 = jnp.bfloat16

D_MODEL = 1024
N_META = 16
CHUNK = 64
N_HEADS = 8
N_KV = 2
HEAD_DIM = 64
D_ATTN = N_HEADS * HEAD_DIM
D_KV = N_KV * HEAD_DIM
WINDOW = 128
PAST_LEN = 4096
ROPE_DIM = HEAD_DIM // 4
ROPE_THETA = 500000.0
D_RNN = D_MODEL
N_RNN_BLOCKS = 8
RNN_BLOCK = D_RNN // N_RNN_BLOCKS
CONV_W = 4
LRU_C = 8.0
N_GROUPS = 4
EXPERTS_PER_GROUP = 4
D_EXPERT = 512
DEPTH = 1
ALPHA = (2 * DEPTH) ** 0.25
LN_EPS = 1e-5

C_Q, C_K, C_V = 0, D_ATTN, D_ATTN + D_KV
C_XR = D_ATTN + 2 * D_KV
C_GR = C_XR + D_RNN
C_GL = C_GR + D_RNN
D_IN = C_GL + 2 * D_MODEL

LANE = 128
SUBLANE = 8
TM = 512
TM_SAMPLE = 256
TD = 1024
TS = 256
DMA_UNROLL = 8
N_PAIR = 6
N_CLASS = N_GROUPS * N_PAIR
ROUTE_ROWS = 32
assert N_GROUPS == 4 and EXPERTS_PER_GROUP == 4 and N_CLASS <= ROUTE_ROWS
ROW_TILE = D_MODEL // LANE
PITCH = ROW_TILE + 1
VMEM_LIMIT = 56 * 1024 * 1024

PAIR_A = np.array([0, 0, 0, 1, 1, 3], np.int32)
PAIR_B = np.array([1, 2, 3, 3, 2, 2], np.int32)


def _params(n_grid_dims=1):
    return pltpu.CompilerParams(
        dimension_semantics=("arbitrary",) * n_grid_dims, vmem_limit_bytes=VMEM_LIMIT)


def _layer_norm(x, g, b):
    mu = jnp.mean(x, axis=-1, keepdims=True)
    xc = x - mu
    var = jnp.mean(xc * xc, axis=-1, keepdims=True)
    return xc * lax.rsqrt(var + LN_EPS) * g + b


def _dot(a, b):
    return jnp.dot(a, b, preferred_element_type=F32)


def _store_token_rows(ref, x, scalars):
    n = x.shape[0]
    for j in range(ROW_TILE):
        ref[pl.ds(j, n, stride=PITCH), :] = x[:, j * LANE:(j + 1) * LANE]
    ref[pl.ds(ROW_TILE, n, stride=PITCH), :] = scalars


def _load_token_rows(ref, n):
    return jnp.concatenate(
        [ref[pl.ds(j, n, stride=PITCH), :] for j in range(ROW_TILE)], axis=1)


def _sigmoid(x):
    return 0.5 * jnp.tanh(0.5 * x) + 0.5


def _inproj_kernel(x_ref, lng_ref, lnb_ref, win_ref, bgate_ref, rc_ref, rs1_ref, rs2_ref,
                   cw_ref, cb_ref, wrg_ref, ba_ref, bx_ref, lam_ref, iconv_ref, ih_ref,
                   q_ref, k_ref, v_ref, rnn_ref, g_ref, conv_ref, h_ref,
                   xr_s, gr_s, xbuf, abuf, ubuf, hcar, *, seg, nseg, tiles_per_stream):
    i = pl.program_id(0)
    slot_proj = i % 2
    slot_rec = 1 - slot_proj

    @pl.when(i == 0)
    def _():
        xr_s[...] = jnp.zeros_like(xr_s)
        gr_s[...] = jnp.zeros_like(gr_s)
        xbuf[...] = jnp.zeros_like(xbuf)
        hcar[...] = jnp.zeros_like(hcar)

    if tiles_per_stream > 1:
        @pl.when((i > 0) & ((i - 1) % tiles_per_stream == 0))
        def _():
            xbuf[0:SUBLANE, :] = iconv_ref[0]
            hcar[...] = ih_ref[0]

    c8 = LRU_C * jax.nn.softplus(-lam_ref[...])
    row = lax.broadcasted_iota(jnp.int32, (SUBLANE, D_RNN), 0)
    cw = cw_ref[...]
    for s in range(nseg):
        r0 = s * seg
        if tiles_per_stream == 1:
            xbuf[0:SUBLANE, :] = iconv_ref[s]
            hcar[...] = ih_ref[s]

        xbuf[SUBLANE:SUBLANE + seg, :] = xr_s[slot_rec, r0:r0 + seg, :]
        xc = (xbuf[SUBLANE:SUBLANE + seg, :] * cw[3:4, :]
              + xbuf[SUBLANE - 1:SUBLANE - 1 + seg, :] * cw[2:3, :]
              + xbuf[SUBLANE - 2:SUBLANE - 2 + seg, :] * cw[1:2, :]
              + xbuf[SUBLANE - 3:SUBLANE - 3 + seg, :] * cw[0:1, :]
              + cb_ref[...])
        tail = xbuf[seg:seg + SUBLANE, :]
        conv_ref[s] = tail
        xbuf[0:SUBLANE, :] = tail

        xcb = xc.astype(BF16)
        blocks = [slice(n * RNN_BLOCK, (n + 1) * RNN_BLOCK) for n in range(N_RNN_BLOCKS)]
        gates = [_dot(xcb[:, sl], wrg_ref[n]) for n, sl in enumerate(blocks)]
        for sl, ri in zip(blocks, gates):
            xcn = xc[:, sl]
            r = _sigmoid(ri[:, :RNN_BLOCK] + ba_ref[:, sl])
            ig = _sigmoid(ri[:, RNN_BLOCK:] + bx_ref[:, sl])
            log_a = -c8[:, sl] * r
            a = jnp.exp(log_a)
            y = -jnp.tanh(log_a) * (a * a + 1.0)
            mult = jnp.where(y > 0.0, y * lax.rsqrt(y), 0.0)
            abuf[0:seg, sl] = a
            ubuf[0:seg, sl] = mult * ig * xcn

        h = hcar[0:1, :]
        for gi in range(seg // SUBLANE):
            o = gi * SUBLANE
            a = abuf[o:o + SUBLANE, :]
            u = ubuf[o:o + SUBLANE, :]
            for sh in (1, 2, 4):
                keep = row >= sh
                u = u + a * jnp.where(keep, pltpu.roll(u, sh, 0), 0.0)
                a = a * jnp.where(keep, pltpu.roll(a, sh, 0), 1.0)
            hg = a * h + u
            ubuf[o:o + SUBLANE, :] = hg
            h = hg[SUBLANE - 1:SUBLANE, :]
        h_b = jnp.broadcast_to(h, (SUBLANE, D_RNN))
        hcar[...] = h_b
        h_ref[s] = h_b
        rnn_ref[r0:r0 + seg, :] = (
            ubuf[0:seg, :] * jax.nn.gelu(gr_s[slot_rec, r0:r0 + seg, :])).astype(BF16)

    zn = _layer_norm(x_ref[...], lng_ref[...], lnb_ref[...]).astype(BF16)
    qkv = _dot(zn, win_ref[:, C_Q:C_XR])
    rc, rs1, rs2 = rc_ref[...], rs1_ref[...], rs2_ref[...]

    def rope(t):
        return t * rc + pltpu.roll(t, 8, 1) * rs1 + pltpu.roll(t, LANE - 8, 1) * rs2

    for j in range(D_ATTN // LANE):
        q_ref[:, j * LANE:(j + 1) * LANE] = (
            rope(qkv[:, j * LANE:(j + 1) * LANE]) * (HEAD_DIM ** -0.5)).astype(BF16)
    k_ref[...] = rope(qkv[:, C_K:C_V])
    v_ref[...] = qkv[:, C_V:C_XR]
    g_ref[...] = _sigmoid(_dot(zn, win_ref[:, C_GL:D_IN]) + bgate_ref[...]).astype(BF16)
    xr_s[slot_proj] = _dot(zn, win_ref[:, C_XR:C_GR])
    gr_s[slot_proj] = _dot(zn, win_ref[:, C_GR:C_GL])


def _inproj_call(x2d, rope_tabs, iconv, ih, wts, *, tm, seg, tiles_per_stream, rope_tiles, name):
    n = x2d.shape[0]
    nt = n // tm
    nseg = tm // seg
    n_streams = n // (seg * tiles_per_stream) if tiles_per_stream > 1 else n // seg
    shared_init = iconv.shape[0] == 1

    def cur(i):
        return jnp.minimum(i, nt - 1)

    def prev(i):
        return jnp.maximum(i - 1, 0)

    def full(a):
        return pl.BlockSpec(a.shape, lambda i: (0,) * a.ndim)

    def rows(w):
        return pl.BlockSpec((tm, w), lambda i: (cur(i), 0))

    rope_spec = pl.BlockSpec((tm, LANE), lambda i: (cur(i) % rope_tiles, 0))
    if shared_init:
        init_spec = pl.BlockSpec((1, SUBLANE, D_RNN), lambda i: (0, 0, 0))
    else:
        init_spec = pl.BlockSpec((nseg, SUBLANE, D_RNN), lambda i: (prev(i), 0, 0))
    state_spec = pl.BlockSpec((nseg, SUBLANE, D_RNN),
                              lambda i: (prev(i) // tiles_per_stream, 0, 0))

    (lng, lnb, win, bgate, cw, cb, wrg, ba, bx, lam) = wts
    kern = functools.partial(_inproj_kernel, seg=seg, nseg=nseg, tiles_per_stream=tiles_per_stream)
    return pl.pallas_call(
        kern,
        grid=(nt + 1,),
        in_specs=[rows(D_MODEL), full(lng), full(lnb), full(win), full(bgate),
                  rope_spec, rope_spec, rope_spec,
                  full(cw), full(cb), full(wrg), full(ba), full(bx), full(lam),
                  init_spec, init_spec],
        out_specs=[rows(D_ATTN), rows(D_KV), rows(D_KV),
                   pl.BlockSpec((tm, D_RNN), lambda i: (prev(i), 0)),
                   rows(2 * D_MODEL), state_spec, state_spec],
        out_shape=[jax.ShapeDtypeStruct((n, D_ATTN), BF16),
                   jax.ShapeDtypeStruct((n, D_KV), F32),
                   jax.ShapeDtypeStruct((n, D_KV), F32),
                   jax.ShapeDtypeStruct((n, D_RNN), BF16),
                   jax.ShapeDtypeStruct((n, 2 * D_MODEL), BF16),
                   jax.ShapeDtypeStruct((n_streams, SUBLANE, D_RNN), F32),
                   jax.ShapeDtypeStruct((n_streams, SUBLANE, D_RNN), F32)],
        scratch_shapes=[pltpu.VMEM((2, tm, D_RNN), F32),
                        pltpu.VMEM((2, tm, D_RNN), F32),
                        pltpu.VMEM((SUBLANE + seg, D_RNN), F32),
                        pltpu.VMEM((seg, D_RNN), F32),
                        pltpu.VMEM((seg, D_RNN), F32),
                        pltpu.VMEM((SUBLANE, D_RNN), F32)],
        compiler_params=_params(),
        name=name,
    )(x2d, lng, lnb, win, bgate, *rope_tabs, cw, cb, wrg, ba, bx, lam, iconv, ih)


HIST = (CONV_W - 1) * SUBLANE
N_SLAB = D_RNN // LANE


def _inproj_tm_kernel(x_ref, lng_ref, lnb_ref, win_ref, bgate_ref, rc_ref, rs1_ref, rs2_ref,
                      cw_ref, cb_ref, wrg_ref, ba_ref, bx_ref, lam_ref, ihist_ref, ih_ref,
                      *rest, tt, n_side):
    side_in = rest[:n_side]
    q_ref, k_ref, v_ref, rnn_ref, g_ref, hist_ref, h_ref = rest[n_side:n_side + 7]
    side_out = rest[n_side + 7:2 * n_side + 7]
    xt_s, gr_s, abuf, ubuf, ibuf, hist, hcar = rest[2 * n_side + 7:]
    for src, dst in zip(side_in, side_out):
        dst[...] = src[...].astype(BF16)
    nb = SUBLANE
    rows = nb * tt
    i = pl.program_id(0)
    slot_proj = i % 2
    slot_rec = 1 - slot_proj

    @pl.when(i == 0)
    def _():
        xt_s[...] = jnp.zeros_like(xt_s)
        gr_s[...] = jnp.zeros_like(gr_s)
        hist[...] = jnp.zeros_like(hist)
        hcar[...] = jnp.zeros_like(hcar)

    @pl.when(i == 1)
    def _():
        hist[...] = ihist_ref[...]
        hcar[...] = ih_ref[...]

    c8 = LRU_C * jax.nn.softplus(-lam_ref[...])
    cw = cw_ref[...]
    for j in range(N_SLAB):
        sl = slice(j * LANE, (j + 1) * LANE)
        xt = xt_s.at[slot_rec, j]
        xt[0:HIST, :] = hist[j]
        xc = (xt[HIST:HIST + rows, :] * cw[3:4, sl]
              + xt[HIST - nb:HIST - nb + rows, :] * cw[2:3, sl]
              + xt[HIST - 2 * nb:HIST - 2 * nb + rows, :] * cw[1:2, sl]
              + xt[0:rows, :] * cw[0:1, sl]
              + cb_ref[:, sl])
        hist[j] = xt[rows:rows + HIST, :]
        ri = _dot(xc.astype(BF16), wrg_ref[j])
        ubuf[j] = xc
        abuf[j] = ri[:, :RNN_BLOCK]
        ibuf[j] = ri[:, RNN_BLOCK:]
    gates_done = jnp.minimum(jnp.abs(ri[0:1, 0:RNN_BLOCK]), 0.0)
    for j in range(N_SLAB):
        sl = slice(j * LANE, (j + 1) * LANE)
        xc = ubuf[j]
        r = _sigmoid(abuf[j] + (ba_ref[:, sl] + gates_done))
        ig = _sigmoid(ibuf[j] + (bx_ref[:, sl] + gates_done))
        log_a = -c8[:, sl] * r
        a = jnp.exp(log_a)
        y = -jnp.tanh(log_a) * (a * a + 1.0)
        mult = jnp.where(y > 0.0, y * lax.rsqrt(y), 0.0)
        abuf[j] = a
        ubuf[j] = mult * ig * xc

    for j in range(N_SLAB):
        h = hcar[j]
        for t in range(tt):
            h = abuf[j, t * nb:(t + 1) * nb, :] * h + ubuf[j, t * nb:(t + 1) * nb, :]
            ubuf[j, t * nb:(t + 1) * nb, :] = h
        hcar[j] = h
    h_ref[...] = hcar[...]
    hist_ref[...] = hist[...]

    for b in range(nb):
        for j in range(N_SLAB):
            sl = slice(j * LANE, (j + 1) * LANE)
            hb = ubuf.at[j][pl.ds(b, tt, stride=nb), :]
            gate = jax.nn.gelu(gr_s[slot_rec, b * tt:(b + 1) * tt, sl])
            rnn_ref[b, :, sl] = (hb * gate).astype(BF16)

    zn = _layer_norm(x_ref[...].reshape(rows, D_MODEL), lng_ref[...], lnb_ref[...]).astype(BF16)
    qkv = _dot(zn, win_ref[:, C_Q:C_XR])
    rc, rs1, rs2 = (jnp.concatenate([t_ref[...]] * nb, axis=0) for t_ref in (rc_ref, rs1_ref, rs2_ref))

    def rope(t):
        return t * rc + pltpu.roll(t, 8, 1) * rs1 + pltpu.roll(t, LANE - 8, 1) * rs2

    qs = [(rope(qkv[:, j * LANE:(j + 1) * LANE]) * (HEAD_DIM ** -0.5)).astype(BF16)
          for j in range(D_ATTN // LANE)]
    kk = rope(qkv[:, C_K:C_V])
    g = _sigmoid(_dot(zn, win_ref[:, C_GL:D_IN]) + bgate_ref[...]).astype(BF16)
    xr = _dot(zn, win_ref[:, C_XR:C_GR])
    gr_s[slot_proj] = _dot(zn, win_ref[:, C_GR:C_GL])
    for b in range(nb):
        rb = slice(b * tt, (b + 1) * tt)
        for j, qj in enumerate(qs):
            q_ref[b, :, j * LANE:(j + 1) * LANE] = qj[rb, :]
        k_ref[b] = kk[rb, :]
        v_ref[b] = qkv[rb, C_V:C_XR]
        g_ref[b] = g[rb, :]
        for j in range(N_SLAB):
            xt_s.at[slot_proj, j][pl.ds(HIST + b, tt, stride=nb), :] = xr[rb, j * LANE:(j + 1) * LANE]


def _inproj_tm_call(x3d, rope_tabs, ihist, ih, wts, expert_w, *, tt, name):
    nb, s, _ = x3d.shape
    assert nb == SUBLANE and s % tt == 0 and tt % (2 * SUBLANE) == 0
    nt = s // tt
    rows = nb * tt
    assert all(w.shape[0] % (nt * 2 * SUBLANE) == 0 for w in expert_w)

    def cur(i):
        return jnp.minimum(i, nt - 1)

    def prev(i):
        return jnp.maximum(i - 1, 0)

    def full(a):
        return pl.BlockSpec(a.shape, lambda i: (0,) * a.ndim)

    def tile(w, which=cur):
        return pl.BlockSpec((nb, tt, w), lambda i: (0, which(i), 0))

    rope_spec = pl.BlockSpec((tt, LANE), lambda i: (cur(i), 0))
    ew_specs = [pl.BlockSpec((w.shape[0] // nt, w.shape[1]), lambda i: (cur(i), 0)) for w in expert_w]
    (lng, lnb, win, bgate, cw, cb, wrg, ba, bx, lam) = wts
    return pl.pallas_call(
        functools.partial(_inproj_tm_kernel, tt=tt, n_side=len(expert_w)),
        grid=(nt + 1,),
        in_specs=[tile(D_MODEL), full(lng), full(lnb), full(win), full(bgate),
                  rope_spec, rope_spec, rope_spec,
                  full(cw), full(cb), full(wrg), full(ba), full(bx), full(lam),
                  full(ihist), full(ih)] + ew_specs,
        out_specs=[tile(D_ATTN), tile(D_KV), tile(D_KV), tile(D_RNN, prev), tile(2 * D_MODEL),
                   full(ihist), full(ih)] + ew_specs,
        out_shape=[jax.ShapeDtypeStruct((nb, s, D_ATTN), BF16),
                   jax.ShapeDtypeStruct((nb, s, D_KV), F32),
                   jax.ShapeDtypeStruct((nb, s, D_KV), F32),
                   jax.ShapeDtypeStruct((nb, s, D_RNN), BF16),
                   jax.ShapeDtypeStruct((nb, s, 2 * D_MODEL), BF16),
                   jax.ShapeDtypeStruct(ihist.shape, F32),
                   jax.ShapeDtypeStruct(ih.shape, F32)]
                  + [jax.ShapeDtypeStruct(w.shape, BF16) for w in expert_w],
        scratch_shapes=[pltpu.VMEM((2, N_SLAB, HIST + rows, LANE), F32),
                        pltpu.VMEM((2, rows, D_RNN), F32),
                        pltpu.VMEM((N_SLAB, rows, LANE), F32),
                        pltpu.VMEM((N_SLAB, rows, LANE), F32),
                        pltpu.VMEM((N_SLAB, rows, LANE), F32),
                        pltpu.VMEM((N_SLAB, HIST, LANE), F32),
                        pltpu.VMEM((N_SLAB, SUBLANE, LANE), F32)],
        compiler_params=_params(),
        name=name,
    )(x3d, lng, lnb, win, bgate, *rope_tabs, cw, cb, wrg, ba, bx, lam, ihist, ih, *expert_w)


N_KEYS = N_META + WINDOW + CHUNK
KEY_PAD = 2 * LANE


def _attn_kernel(sink_ref, q_ref, km_ref, vm_ref, kp_ref, vp_ref, ko_ref, vo_ref, o_ref,
                 *, nchunk, tiles_per_stream, mask_first, separate):
    i = pl.program_id(0)
    first = (i % tiles_per_stream) == 0
    lane = lax.broadcasted_iota(jnp.int32, (1, LANE), 1)
    lo = lane < HEAD_DIM

    def slabs(x):
        xs = pltpu.roll(x, HEAD_DIM, 1)
        return (jnp.where(lo, x, xs).astype(BF16), jnp.where(lo, xs, x).astype(BF16))

    km, vm = slabs(km_ref[...]), slabs(vm_ref[...])
    kp, vp = slabs(kp_ref[...]), slabs(vp_ref[...])
    ko, vo = slabs(ko_ref[...]), slabs(vo_ref[...])
    zpad = jnp.zeros((KEY_PAD - N_KEYS, LANE), BF16)
    kpos = lax.broadcasted_iota(jnp.int32, (1, KEY_PAD), 1)
    qrow = lax.broadcasted_iota(jnp.int32, (4 * CHUNK, 1), 0)
    q_per_kv = N_HEADS // N_KV

    def keys(meta, prev, own, c):
        if separate:
            return [meta[N_META * c:N_META * (c + 1)], prev[WINDOW * c:WINDOW * (c + 1)],
                    own[CHUNK * c:CHUNK * (c + 1)]]
        parts = [meta]
        if CHUNK * c < WINDOW:
            parts.append(prev[CHUNK * c:WINDOW])
        parts.append(own[max(CHUNK * c - WINDOW, 0):CHUNK * (c + 1)])
        return parts

    for c in range(nchunk):
        qc = q_ref[c * CHUNK:(c + 1) * CHUNK, :]
        for kv in range(N_KV):
            kc = jnp.concatenate(keys(km[kv], kp[kv], ko[kv], c) + [zpad], axis=0)
            vc = jnp.concatenate(keys(vm[kv], vp[kv], vo[kv], c) + [zpad], axis=0)
            qs = []
            for g in range(q_per_kv):
                h = kv * q_per_kv + g
                slab = qc[:, (h // 2) * LANE:(h // 2 + 1) * LANE]
                keep = lo if h % 2 == 0 else jnp.logical_not(lo)
                qs.append(jnp.where(keep, slab, jnp.zeros_like(slab)))
            qst = jnp.concatenate(qs, axis=0)
            s = lax.dot_general(qst, kc, (((1,), (1,)), ((), ())), preferred_element_type=F32)
            bad = kpos >= N_KEYS
            n_masked = WINDOW - CHUNK * c
            if mask_first and n_masked > 0:
                lim = N_META + jnp.where(first, n_masked, 0)
                bad = bad | ((kpos >= N_META) & (kpos < lim))
            s = jnp.where(bad, -jnp.inf, s)
            sink = jnp.full((4 * CHUNK, 1), sink_ref[kv * q_per_kv + q_per_kv - 1], F32)
            for g in range(q_per_kv - 2, -1, -1):
                sink = jnp.where(qrow < (g + 1) * CHUNK, sink_ref[kv * q_per_kv + g], sink)
            m = jnp.maximum(jnp.max(s, axis=-1, keepdims=True), sink)
            p = jnp.exp(s - m)
            denom = jnp.sum(p, axis=-1, keepdims=True) + jnp.exp(sink - m)
            o = _dot(p.astype(BF16), vc) / denom
            for jj in range(q_per_kv // 2):
                g0 = 2 * jj
                oslab = jnp.where(lo, o[g0 * CHUNK:(g0 + 1) * CHUNK, :],
                                  o[(g0 + 1) * CHUNK:(g0 + 2) * CHUNK, :])
                j = kv * (q_per_kv // 2) + jj
                o_ref[c * CHUNK:(c + 1) * CHUNK, j * LANE:(j + 1) * LANE] = oslab.astype(BF16)


def _attn_call(sinks, q, kmeta, vmeta, kprev, vprev, kown, vown, *, tq, tiles_per_stream,
               mask_first, prev_map, meta_map, name, separate=False):
    n = q.shape[0]
    nt = n // tq
    nchunk = tq // CHUNK
    kern = functools.partial(_attn_kernel, nchunk=nchunk, tiles_per_stream=tiles_per_stream,
                             mask_first=mask_first, separate=separate)
    per_tile = nchunk if separate else 1
    meta_spec = pl.BlockSpec((per_tile * N_META, D_KV), meta_map)
    prev_spec = pl.BlockSpec((per_tile * WINDOW, D_KV), prev_map)
    own_spec = pl.BlockSpec((tq, D_KV), lambda i: (i, 0))
    return pl.pallas_call(
        kern,
        grid=(nt,),
        in_specs=[pl.BlockSpec(memory_space=pltpu.SMEM),
                  pl.BlockSpec((tq, D_ATTN), lambda i: (i, 0)),
                  meta_spec, meta_spec, prev_spec, prev_spec, own_spec, own_spec],
        out_specs=pl.BlockSpec((tq, D_ATTN), lambda i: (i, 0)),
        out_shape=jax.ShapeDtypeStruct((n, D_ATTN), BF16),
        compiler_params=_params(),
        name=name,
    )(sinks, q, kmeta, vmeta, kprev, vprev, kown, vown)


def _merge_kernel(o_ref, rnn_ref, g_ref, x_ref, lng_ref, lnb_ref, wpa_ref, wpr_ref, wo_ref,
                  l1g_ref, l1b_ref, wr_ref, br_ref, triu_ref, cin_ref,
                  z1_ref, z1t_ref, slot_ref, cout_ref, cnt):
    i = pl.program_id(0)

    @pl.when(i == 0)
    def _():
        cnt[...] = jnp.concatenate([cin_ref[...]] * (cnt.shape[1] // LANE), axis=1)

    g = g_ref[...]
    pa = _dot(o_ref[...], wpa_ref[...])
    pr = _dot(rnn_ref[...], wpr_ref[...])
    mixed = g[:, :D_MODEL].astype(F32) * pa + g[:, D_MODEL:].astype(F32) * pr
    mix = _dot(mixed.astype(BF16), wo_ref[...])
    zn = _layer_norm(x_ref[...], lng_ref[...], lnb_ref[...])
    z1 = _layer_norm(ALPHA * zn + mix, l1g_ref[...], l1b_ref[...])
    z1_ref[...] = z1

    z_hi = z1.astype(BF16)
    z_lo = (z1 - z_hi.astype(F32)).astype(BF16)
    la = _dot(z_hi, wr_ref[...])
    lb = _dot(z_lo, wr_ref[...])
    logit = la[:, :LANE] + la[:, LANE:] + lb[:, :LANE] + lb[:, LANE:] + br_ref[...]

    tm = logit.shape[0]
    lt = logit.T[0:ROUTE_ROWS, :]
    row = lax.broadcasted_iota(jnp.int32, (ROUTE_ROWS, tm), 0)
    rin = (row & 3).astype(F32)
    seg = (row >> 2).astype(F32)
    odd1 = (row & 1) == 1
    odd2 = (row & 2) == 2
    ninf = -jnp.inf

    def seg_reduce(x, op):
        y = op(x, jnp.where(odd1, pltpu.roll(x, 1, 0), pltpu.roll(x, ROUTE_ROWS - 1, 0)))
        return op(y, jnp.where(odd2, pltpu.roll(y, 2, 0), pltpu.roll(y, ROUTE_ROWS - 2, 0)))

    xl = jnp.where(row < N_GROUPS * (1 + EXPERTS_PER_GROUP), lt, ninf)
    m1 = seg_reduce(xl, jnp.maximum)
    i1 = seg_reduce(jnp.where(xl == m1, rin, 4.0), jnp.minimum)
    xl2 = jnp.where(rin == i1, ninf, xl)
    m2 = seg_reduce(xl2, jnp.maximum)
    i2 = seg_reduce(jnp.where(xl2 == m2, rin, 4.0), jnp.minimum)
    esum = seg_reduce(jnp.where(row < N_GROUPS, jnp.exp(xl - m1), 0.0), jnp.add)
    gidx = i1[0:1, :]
    gw = 1.0 / esum[0:1, :]
    chosen = seg == gidx + 1.0

    def pick(x):
        return jnp.max(jnp.where(chosen, x, ninf), axis=0, keepdims=True)

    v1, v2, l1, l2 = pick(m1), pick(m2), pick(i1), pick(i2)
    e21 = jnp.exp(v2 - v1)
    w1 = gw / (1.0 + e21)
    w2 = gw * e21 / (1.0 + e21)
    first_lo = l1 < l2
    ea, eb = jnp.minimum(l1, l2), jnp.maximum(l1, l2)
    w_lo = jnp.where(first_lo, w1, w2)
    w_hi = jnp.where(first_lo, w2, w1)
    lex = ea * (7.0 - ea) * 0.5 + (eb - ea - 1.0)
    pair = jnp.where(lex == 3.0, 4.0, jnp.where(lex == 4.0, 3.0, lex))
    flipped = lex == 5.0
    wa = jnp.where(flipped, w_hi, w_lo)
    wb = jnp.where(flipped, w_lo, w_hi)
    cls = gidx * N_PAIR + pair

    onehot = row.astype(F32) == cls
    ohf = jnp.where(onehot, 1.0, 0.0)
    before = _dot(ohf.astype(BF16), triu_ref[...])
    rank = jnp.sum(jnp.where(onehot, before + cnt[...], 0.0), axis=0, keepdims=True)
    cnt[...] = cnt[...] + jnp.sum(ohf, axis=1, keepdims=True)
    cout_ref[...] = cnt[:, 0:LANE]
    slot_ref[0] = jnp.broadcast_to(cls, (SUBLANE, tm))
    slot_ref[1] = jnp.broadcast_to(rank, (SUBLANE, tm))

    srow = lax.broadcasted_iota(jnp.int32, (LANE, tm), 0)
    scalars = jnp.where(srow == 1, wa, jnp.where(srow == 2, wb, 0.0)).T
    _store_token_rows(z1t_ref, z1, scalars)


def _merge_call(o, rnn, g, x2d, counts_in, wts, *, name):
    n = x2d.shape[0]
    nt = n // TM

    def full(a):
        return pl.BlockSpec(a.shape, lambda i: (0,) * a.ndim)

    def rows(w):
        return pl.BlockSpec((TM, w), lambda i: (i, 0))

    return pl.pallas_call(
        _merge_kernel,
        grid=(nt,),
        in_specs=[rows(D_ATTN), rows(D_RNN), rows(2 * D_MODEL), rows(D_MODEL)]
                 + [full(w) for w in wts] + [full(counts_in)],
        out_specs=[rows(D_MODEL), pl.BlockSpec((TM * PITCH, LANE), lambda i: (i, 0)),
                   pl.BlockSpec((None, 2, SUBLANE, TM), lambda i: (i, 0, 0, 0)),
                   pl.BlockSpec((ROUTE_ROWS, LANE), lambda i: (0, 0))],
        out_shape=[jax.ShapeDtypeStruct((n, D_MODEL), F32),
                   jax.ShapeDtypeStruct((n * PITCH, LANE), F32),
                   jax.ShapeDtypeStruct((nt, 2, SUBLANE, TM), F32),
                   jax.ShapeDtypeStruct((ROUTE_ROWS, LANE), F32)],
        scratch_shapes=[pltpu.VMEM((ROUTE_ROWS, TM), F32)],
        compiler_params=_params(),
        name=name,
    )(o, rnn, g, x2d, *wts, counts_in)


def _plan_kernel(cnt_ref, slot_p_ref, slot_s_ref, pos_p_ref, pos_s_ref, tiles_ref):
    shift = TS.bit_length() - 1
    off, end, fill, fill_p = [], [], [], []
    run = jnp.int32(0)
    last = jnp.int32(0)
    for c in range(N_CLASS):
        n = cnt_ref[c]
        off.append(run)
        fill.append(run + n)
        fill_p.append(run + cnt_ref[N_CLASS + c])
        run = run + lax.shift_left(lax.shift_right_logical(n + (TS - 1), shift), shift)
        end.append(run)
        last = jnp.where(n > 0, c, last)

    def slots(slot_ref, pos_ref):
        cls = slot_ref[:, 0]
        o = jnp.zeros(cls.shape, jnp.int32)
        for c in range(N_CLASS):
            o = jnp.where(cls == float(c), off[c], o)
        pos_ref[...] = ((o + slot_ref[:, 1].astype(jnp.int32)) * PITCH)[:, 0:1, :]

    slots(slot_p_ref, pos_p_ref)
    slots(slot_s_ref, pos_s_ref)

    tstart = lax.broadcasted_iota(jnp.int32, (SUBLANE, LANE), 1) * TS
    tcls = jnp.zeros((SUBLANE, LANE), jnp.int32)
    for c in range(N_CLASS):
        tcls = tcls + jnp.where(tstart >= end[c], 1, 0)
    tcls = jnp.minimum(tcls, last)
    zero = jnp.zeros((SUBLANE, LANE), jnp.int32)
    grp, ea, eb, rows, rows_p = zero, zero, zero, zero, zero
    for c in range(N_CLASS):
        sel = tcls == c
        first = (c // N_PAIR) * EXPERTS_PER_GROUP
        grp = jnp.where(sel, c // N_PAIR, grp)
        ea = jnp.where(sel, first + int(PAIR_A[c % N_PAIR]), ea)
        eb = jnp.where(sel, first + int(PAIR_B[c % N_PAIR]), eb)
        rows = jnp.where(sel, fill[c], rows)
        rows_p = jnp.where(sel, fill_p[c], rows_p)
    rows = jnp.clip(rows - tstart, 0, TS)
    rows_p = jnp.clip(rows_p - tstart, 0, TS)
    sub = lax.broadcasted_iota(jnp.int32, (SUBLANE, LANE), 0)
    tiles_ref[...] = jnp.where(sub == 0, grp, jnp.where(sub == 1, ea, jnp.where(
        sub == 2, eb, jnp.where(sub == 3, rows, rows_p))))


def _plan_call(counts, slot_p, slot_s):
    def full(a):
        return pl.BlockSpec(a.shape, lambda i: (0,) * a.ndim)

    pos_shapes = [jax.ShapeDtypeStruct((a.shape[0], 1, TM), jnp.int32) for a in (slot_p, slot_s)]
    tiles_shape = jax.ShapeDtypeStruct((SUBLANE, LANE), jnp.int32)
    return pl.pallas_call(
        _plan_kernel,
        grid=(1,),
        in_specs=[pl.BlockSpec(memory_space=pltpu.SMEM), full(slot_p), full(slot_s)],
        out_specs=[full(s) for s in pos_shapes] + [full(tiles_shape)],
        out_shape=pos_shapes + [tiles_shape],
        compiler_params=_params(),
        name="slot_plan",
    )(counts, slot_p, slot_s)


def _dispatch_kernel(pos_ref, extra_ref, z1t_ref, xs_ref, rowbuf, sem, *, nt, fresh):
    i = pl.program_id(0)
    slot = i % 2
    tile_rows = TS * PITCH

    def row_copy(sl, r, p):
        return pltpu.make_async_copy(rowbuf.at[sl, pl.ds(r * PITCH, PITCH), :],
                                     xs_ref.at[pl.ds(p, PITCH), :], sem.at[sl])

    def drain(sl):
        pltpu.make_async_copy(rowbuf.at[sl], xs_ref.at[pl.ds(0, TD * PITCH), :], sem.at[sl]).wait()

    if fresh:
        @pl.when(i == 0)
        def _():
            rowbuf[1, 0:tile_rows, :] = jnp.zeros((tile_rows, LANE), F32)
            n_tiles = extra_ref.shape[0]

            def fill_copy(t):
                dst = pl.multiple_of(t * tile_rows, tile_rows)
                return pltpu.make_async_copy(rowbuf.at[1, pl.ds(0, tile_rows), :],
                                             xs_ref.at[pl.ds(dst, tile_rows), :], sem.at[2])

            def fill(t, c):
                @pl.when(extra_ref[t] < TS)
                def _():
                    fill_copy(t).start()
                return c
            lax.fori_loop(0, n_tiles, fill, 0)

            def fill_wait(t, c):
                @pl.when(extra_ref[t] < TS)
                def _():
                    fill_copy(0).wait()
                return c
            lax.fori_loop(0, n_tiles, fill_wait, 0)

    @pl.when(i >= 2)
    def _():
        drain(slot)

    rowbuf[slot] = z1t_ref[...]

    def issue(rb, c):
        for u in range(DMA_UNROLL):
            r = rb * DMA_UNROLL + u
            row_copy(slot, r, pos_ref[0, r]).start(priority=u % 2)
        return c
    lax.fori_loop(0, TD // DMA_UNROLL, issue, 0)

    @pl.when(i == nt - 1)
    def _():
        drain(slot)
        if nt > 1:
            drain(1 - slot)


def _dispatch_call(pos3, z1t, xs, tile_rows, *, name):
    nt = z1t.shape[0] // (TD * PITCH)
    fresh = xs is None
    n_slots = tile_rows.shape[0] * TS if fresh else xs.shape[0] // PITCH
    extra = tile_rows if fresh else xs
    extra_spec = pl.BlockSpec(memory_space=pltpu.SMEM if fresh else pl.ANY)
    return pl.pallas_call(
        functools.partial(_dispatch_kernel, nt=nt, fresh=fresh),
        grid=(nt,),
        in_specs=[pl.BlockSpec((None, 1, TD), lambda i: (i, 0, 0), memory_space=pltpu.SMEM),
                  extra_spec,
                  pl.BlockSpec((TD * PITCH, LANE), lambda i: (i, 0))],
        out_specs=pl.BlockSpec(memory_space=pl.ANY),
        out_shape=jax.ShapeDtypeStruct((n_slots * PITCH, LANE), F32),
        input_output_aliases={} if fresh else {1: 0},
        scratch_shapes=[pltpu.VMEM((2, TD * PITCH, LANE), F32), pltpu.SemaphoreType.DMA((3,))],
        compiler_params=_params(),
        name=name,
    )(pos3, extra, z1t)


def _moe_kernel(grp_ref, ea_ref, eb_ref, val_ref, xs_ref, wg_ref, wu_ref, wd_ref, ys_ref):
    del grp_ref
    i = pl.program_id(0)

    @pl.when(val_ref[i] > 0)
    def _():
        xb = _load_token_rows(xs_ref, TS).astype(BF16)
        scalars = xs_ref[pl.ds(ROW_TILE, TS, stride=PITCH), :]
        ea = ea_ref[i] % EXPERTS_PER_GROUP
        eb = eb_ref[i] % EXPERTS_PER_GROUP
        ga, ua = _dot(xb, wg_ref[ea]), _dot(xb, wu_ref[ea])
        gb, ub = _dot(xb, wg_ref[eb]), _dot(xb, wu_ref[eb])
        ya = _dot((jax.nn.silu(ga) * ua).astype(BF16), wd_ref[ea])
        yb = _dot((jax.nn.silu(gb) * ub).astype(BF16), wd_ref[eb])
        y = scalars[:, 1:2] * ya + scalars[:, 2:3] * yb
        _store_token_rows(ys_ref, y, jnp.zeros((TS, LANE), F32))

    @pl.when(val_ref[i] == 0)
    def _():
        ys_ref[...] = jnp.zeros_like(ys_ref)


def _moe_call(tile_grp, tile_ea, tile_eb, tile_valid, xs, wg, wu, wd):
    n_tiles = xs.shape[0] // (TS * PITCH)
    wspec_in = pl.BlockSpec((None, EXPERTS_PER_GROUP, D_MODEL, D_EXPERT),
                            lambda i, grp, *_: (grp[i], 0, 0, 0))
    wspec_out = pl.BlockSpec((None, EXPERTS_PER_GROUP, D_EXPERT, D_MODEL),
                             lambda i, grp, *_: (grp[i], 0, 0, 0))
    grid_spec = pltpu.PrefetchScalarGridSpec(
        num_scalar_prefetch=4,
        grid=(n_tiles,),
        in_specs=[pl.BlockSpec((TS * PITCH, LANE), lambda i, *_: (i, 0)),
                  wspec_in, wspec_in, wspec_out],
        out_specs=pl.BlockSpec((TS * PITCH, LANE), lambda i, *_: (i, 0)),
    )
    return pl.pallas_call(
        _moe_kernel,
        grid_spec=grid_spec,
        out_shape=jax.ShapeDtypeStruct((n_tiles * TS * PITCH, LANE), F32),
        compiler_params=_params(),
        name="moe_experts",
    )(tile_grp, tile_ea, tile_eb, tile_valid, xs, wg, wu, wd)


def _combine_kernel(pos_ref, posn_ref, z1_ref, l2g_ref, l2b_ref, ys_ref, out_ref, ybuf, sem, *, nt):
    i = pl.program_id(0)
    slot = i % 2

    def row_copy(sl, r, p):
        return pltpu.make_async_copy(ys_ref.at[pl.ds(p, ROW_TILE), :],
                                     ybuf.at[sl, pl.ds(r * PITCH, ROW_TILE), :], sem.at[sl])

    def issue(sl, pref):
        def body(rb, c):
            for u in range(DMA_UNROLL):
                r = rb * DMA_UNROLL + u
                row_copy(sl, r, pref[0, r]).start(priority=u % 2)
            return c
        lax.fori_loop(0, TM // DMA_UNROLL, body, 0)

    @pl.when(i == 0)
    def _():
        issue(0, pos_ref)

    @pl.when(i + 1 < nt)
    def _():
        issue(1 - slot, posn_ref)

    pltpu.make_async_copy(ys_ref.at[pl.ds(0, TM * ROW_TILE), :],
                          ybuf.at[slot, pl.ds(0, TM * ROW_TILE), :], sem.at[slot]).wait()

    moe = _load_token_rows(ybuf.at[slot], TM)
    out_ref[...] = _layer_norm(ALPHA * z1_ref[...] + moe, l2g_ref[...], l2b_ref[...])


def _combine_call(pos3, z1, l2g, l2b, ys, *, name):
    n = z1.shape[0]
    nt = n // TM
    return pl.pallas_call(
        functools.partial(_combine_kernel, nt=nt),
        grid=(nt,),
        in_specs=[pl.BlockSpec((None, 1, TM), lambda i: (i, 0, 0), memory_space=pltpu.SMEM),
                  pl.BlockSpec((None, 1, TM), lambda i: (jnp.minimum(i + 1, nt - 1), 0, 0),
                               memory_space=pltpu.SMEM),
                  pl.BlockSpec((TM, D_MODEL), lambda i: (i, 0)),
                  pl.BlockSpec(l2g.shape, lambda i: (0, 0)),
                  pl.BlockSpec(l2b.shape, lambda i: (0, 0)),
                  pl.BlockSpec(memory_space=pl.ANY)],
        out_specs=pl.BlockSpec((TM, D_MODEL), lambda i: (i, 0)),
        out_shape=jax.ShapeDtypeStruct((n, D_MODEL), F32),
        scratch_shapes=[pltpu.VMEM((2, TM * PITCH, LANE), F32), pltpu.SemaphoreType.DMA((2,))],
        compiler_params=_params(),
        name=name,
    )(pos3, pos3, z1, l2g, l2b, ys)


def _rope_tables(pos, reps=1):
    half = ROPE_DIM // 2
    inv_freq = ROPE_THETA ** (-np.arange(half, dtype=np.float64) / half)
    ang = pos.astype(np.float64)[:, None] * inv_freq[None, :]
    cos, sin = np.cos(ang), np.sin(ang)
    p = pos.shape[0]
    ones = np.ones((p, HEAD_DIM - ROPE_DIM))
    zeros = np.zeros((p, HEAD_DIM - ROPE_DIM))
    zh = np.zeros((p, half))
    c = np.concatenate([cos, cos, ones], axis=1)
    s1 = np.concatenate([zh, sin, zeros], axis=1)
    s2 = np.concatenate([-sin, zh, zeros], axis=1)
    rep = LANE // HEAD_DIM
    return tuple(jnp.asarray(np.tile(t, (reps, rep)).astype(np.float32)) for t in (c, s1, s2))


def kernel(x_prompt, x_sample, cache_meta_k, cache_meta_v, cache_win_k, cache_win_v, state_conv, state_h, meta_tokens, ln_in_g, ln_in_b, w_in, b_gate, attn_sinks, conv_w, conv_b, w_rg_a, b_rg_a, w_rg_x, b_rg_x, lru_lambda, w_branch_attn, w_branch_rnn, w_out, ln1_g, ln1_b, w_group, b_group, w_router, b_router, w_gate, w_up, w_down, ln2_g, ln2_b):
    B, S, _ = x_prompt.shape
    DB, TSQ, _ = x_sample.shape
    assert DEPTH == 1 and w_in.shape == (DEPTH, D_MODEL, D_IN)
    assert S % TM == 0 and (DB * TSQ) % TM == 0 and TM % TSQ == 0 and TSQ == CHUNK
    assert cache_win_k.shape[2] == WINDOW and S >= WINDOW
    l = 0

    def row(v):
        return v.reshape(1, -1).astype(F32)

    lng, lnb = row(ln_in_g), row(ln_in_b)
    win = w_in[l].astype(BF16)
    cw = jnp.concatenate([conv_w[l], jnp.zeros((SUBLANE - CONV_W, D_RNN), F32)], axis=0)
    wrg = jnp.concatenate([w_rg_a[l], w_rg_x[l]], axis=-1).astype(BF16)
    in_wts = (lng, lnb, win, row(b_gate[l]), cw, row(conv_b[l]), wrg,
              row(b_rg_a[l]), row(b_rg_x[l]), row(lru_lambda[l]))
    wr = jnp.concatenate([w_group[l], w_router[l].reshape(D_MODEL, -1)], axis=1)
    wr = jnp.pad(wr, ((0, 0), (0, LANE - wr.shape[1])))
    wr_hi = wr.astype(BF16)
    wr_lo = (wr - wr_hi.astype(F32)).astype(BF16)
    wr_cat = jnp.concatenate([wr_hi, wr_lo], axis=1)
    br = jnp.pad(jnp.concatenate([b_group[l], b_router[l].reshape(-1)]),
                 (0, LANE - N_GROUPS * (1 + EXPERTS_PER_GROUP))).reshape(1, LANE)
    merge_small = (row(ln1_g[l]), row(ln1_b[l]), wr_cat, br,
                   jnp.asarray(np.triu(np.ones((TM, TM), np.float32), 1), BF16))
    sinks = attn_sinks[l].astype(F32)

    rope_meta = _rope_tables(np.arange(N_META))
    rope_p = _rope_tables(N_META + np.arange(S))
    rope_s = _rope_tables(N_META + PAST_LEN + np.arange(TSQ), reps=TM_SAMPLE // TSQ)

    xp = x_prompt.reshape(B * S, D_MODEL)
    xs_ = x_sample.reshape(DB * TSQ, D_MODEL)
    n_p, n_s = B * S, DB * TSQ
    zero_state = jnp.zeros((1, SUBLANE, D_RNN), F32)

    _, k_m, v_m, _, _, conv_m, h_m = _inproj_call(
        meta_tokens.astype(F32), rope_meta, zero_state, zero_state, in_wts,
        tm=N_META, seg=N_META, tiles_per_stream=1, rope_tiles=1, name="inproj_meta")
    n_hist = CONV_W - 1
    ihist = conv_m[0, SUBLANE - n_hist:, :].reshape(n_hist, N_SLAB, LANE).transpose(1, 0, 2)
    ihist = jnp.broadcast_to(ihist[:, :, None, :], (N_SLAB, n_hist, B, LANE)).reshape(N_SLAB, HIST, LANE)
    ih = jnp.broadcast_to(h_m[0, 0].reshape(N_SLAB, 1, LANE), (N_SLAB, B, LANE))
    expert_w = tuple(w[l].astype(F32).reshape(-1, w.shape[-1])
                     for w in (w_gate, w_up, w_down, w_branch_attn, w_branch_rnn, w_out))
    q_p, k_p, v_p, rnn_p, g_p, hist_p, hfin_p, wg, wu, wd, wpa, wpr, wo = (
        a.reshape(n_p, -1) if a.shape[:2] == (B, S) else a
        for a in _inproj_tm_call(x_prompt.astype(F32), rope_p, ihist, ih, in_wts, expert_w,
                                 tt=TM // B, name="inproj_prompt"))
    wg = wg.reshape(N_GROUPS, EXPERTS_PER_GROUP, D_MODEL, D_EXPERT)
    wu = wu.reshape(N_GROUPS, EXPERTS_PER_GROUP, D_MODEL, D_EXPERT)
    wd = wd.reshape(N_GROUPS, EXPERTS_PER_GROUP, D_EXPERT, D_MODEL)
    merge_wts = (lng, lnb, wpa, wpr, wo) + merge_small
    iconv_s =jnp.pad(state_conv[l].astype(F32), ((0, 0), (SUBLANE - (CONV_W - 1), 0), (0, 0)))
    ih_s = jnp.broadcast_to(state_h[l].astype(F32)[:, None, :], (DB, SUBLANE, D_RNN))
    q_s, k_s, v_s, rnn_s, g_s, conv_s, h_s = _inproj_call(
        xs_, rope_s, iconv_s, ih_s, in_wts,
        tm=TM_SAMPLE, seg=TSQ, tiles_per_stream=1, rope_tiles=1, name="inproj_sample")

    tq_p = 4 * TM
    assert S % tq_p == 0
    tps = S // tq_p
    win_blocks = tq_p // WINDOW
    o_p = _attn_call(
        sinks, q_p, k_m, v_m, k_p, v_p, k_p, v_p, tq=tq_p, tiles_per_stream=tps, mask_first=True,
        prev_map=lambda i: (jnp.maximum(i * win_blocks - 1, 0), 0),
        meta_map=lambda i: (0, 0), name="attn_prompt")
    cmk = cache_meta_k[l].reshape(DB * N_META, D_KV).astype(F32)
    cmv = cache_meta_v[l].reshape(DB * N_META, D_KV).astype(F32)
    cwk = cache_win_k[l].reshape(DB * WINDOW, D_KV).astype(F32)
    cwv = cache_win_v[l].reshape(DB * WINDOW, D_KV).astype(F32)
    o_s = _attn_call(
        sinks, q_s, cmk, cmv, cwk, cwv, k_s, v_s, tq=4 * TSQ, tiles_per_stream=1, mask_first=False,
        prev_map=lambda i: (i, 0), meta_map=lambda i: (i, 0), name="attn_sample", separate=True)

    zero_cnt = jnp.zeros((ROUTE_ROWS, LANE), F32)
    z1_p, z1t_p, slot_p, cnt_p = _merge_call(
        o_p, rnn_p, g_p, xp, zero_cnt, merge_wts, name="merge_prompt")
    z1_s, z1t_s, slot_s, cnt_all = _merge_call(
        o_s, rnn_s, g_s, xs_, cnt_p, merge_wts, name="merge_sample")

    n_tiles = (n_p + n_s) // TS + N_CLASS
    assert n_tiles <= LANE
    counts = jnp.concatenate([cnt_all[:N_CLASS, 0], cnt_p[:N_CLASS, 0]]).astype(jnp.int32)
    pos_p, pos_s, tiles = _plan_call(counts, slot_p, slot_s)
    tile_grp, tile_ea, tile_eb, tile_rows, tile_rows_p = (tiles[r, :n_tiles] for r in range(5))

    xs1 = _dispatch_call(pos_p.reshape(-1, 1, TD), z1t_p, None, tile_rows_p, name="dispatch_prompt")
    xs2 = _dispatch_call(pos_s.reshape(-1, 1, TD), z1t_s, xs1, tile_rows, name="dispatch_sample")
    ys = _moe_call(tile_grp, tile_ea, tile_eb, tile_rows, xs2, wg, wu, wd)
    l2g, l2b = row(ln2_g[l]), row(ln2_b[l])
    y_p = _combine_call(pos_p, z1_p, l2g, l2b, ys, name="combine_prompt")
    y_s = _combine_call(pos_s, z1_s, l2g, l2b, ys, name="combine_sample")

    def heads(a, lead):
        return a.reshape(*lead, N_KV, HEAD_DIM)[None]

    tail = slice(SUBLANE - (CONV_W - 1), SUBLANE)
    y_prompt = y_p.reshape(B, S, D_MODEL)
    y_sample = y_s.reshape(DB, TSQ, D_MODEL)
    meta_k_p = jnp.broadcast_to(heads(k_m, (1, N_META)), (1, B, N_META, N_KV, HEAD_DIM))
    meta_v_p = jnp.broadcast_to(heads(v_m, (1, N_META)), (1, B, N_META, N_KV, HEAD_DIM))
    win_k_p = heads(k_p.reshape(B, S, D_KV)[:, S - WINDOW:], (B, WINDOW))
    win_v_p = heads(v_p.reshape(B, S, D_KV)[:, S - WINDOW:], (B, WINDOW))
    conv_out_p = hist_p.reshape(N_SLAB, n_hist, B, LANE).transpose(2, 1, 0, 3).reshape(
        1, B, n_hist, D_RNN)
    h_out_p = hfin_p.transpose(1, 0, 2).reshape(1, B, D_RNN)
    new_k_s = heads(k_s, (DB, TSQ))
    new_v_s = heads(v_s, (DB, TSQ))
    conv_out_s = conv_s[:, tail, :][None]
    h_out_s = h_s[:, 0, :][None].astype(state_h.dtype)
    return (y_prompt, y_sample, meta_k_p, meta_v_p, win_k_p, win_v_p, conv_out_p, h_out_p,
            new_k_s, new_v_s, conv_out_s, h_out_s)
```

```python
import functools

import numpy as np
import jax
import jax.numpy as jnp
from jax import lax
from jax.experimental import pallas as pl
from jax.experimental.pallas import tpu as pltpu

F32 = jnp.float32
BF16 = jnp.bfloat16

D_MODEL = 1024
N_META = 16
CHUNK = 64
N_HEADS = 8
N_KV = 2
HEAD_DIM = 64
D_ATTN = N_HEADS * HEAD_DIM
D_KV = N_KV * HEAD_DIM
WINDOW = 128
PAST_LEN = 4096
ROPE_DIM = HEAD_DIM // 4
ROPE_THETA = 500000.0
D_RNN = D_MODEL
N_RNN_BLOCKS = 8
RNN_BLOCK = D_RNN // N_RNN_BLOCKS
CONV_W = 4
LRU_C = 8.0
N_GROUPS = 4
EXPERTS_PER_GROUP = 4
D_EXPERT = 512
DEPTH = 1
ALPHA = (2 * DEPTH) ** 0.25
LN_EPS = 1e-5

C_Q, C_K, C_V = 0, D_ATTN, D_ATTN + D_KV
C_XR = D_ATTN + 2 * D_KV
C_GR = C_XR + D_RNN
C_GL = C_GR + D_RNN
D_IN = C_GL + 2 * D_MODEL

LANE = 128
SUBLANE = 8
TM = 512
TM_SAMPLE = 256
TD = 1024
TS = 256
DMA_UNROLL = 8
N_PAIR = 6
N_CLASS = N_GROUPS * N_PAIR
ROUTE_ROWS = 32
assert N_GROUPS == 4 and EXPERTS_PER_GROUP == 4 and N_CLASS <= ROUTE_ROWS
ROW_TILE = D_MODEL // LANE
PITCH = ROW_TILE + 1
VMEM_LIMIT = 56 * 1024 * 1024

PAIR_A = np.array([0, 0, 0, 1, 1, 3], np.int32)
PAIR_B = np.array([1, 2, 3, 3, 2, 2], np.int32)


def _params(n_grid_dims=1):
    return pltpu.CompilerParams(
        dimension_semantics=("arbitrary",) * n_grid_dims, vmem_limit_bytes=VMEM_LIMIT)


def _layer_norm(x, g, b):
    mu = jnp.mean(x, axis=-1, keepdims=True)
    xc = x - mu
    var = jnp.mean(xc * xc, axis=-1, keepdims=True)
    return xc * lax.rsqrt(var + LN_EPS) * g + b


def _dot(a, b):
    return jnp.dot(a, b, preferred_element_type=F32)


def _store_token_rows(ref, x, scalars):
    n = x.shape[0]
    for j in range(ROW_TILE):
        ref[pl.ds(j, n, stride=PITCH), :] = x[:, j * LANE:(j + 1) * LANE]
    ref[pl.ds(ROW_TILE, n, stride=PITCH), :] = scalars


def _load_token_rows(ref, n):
    return jnp.concatenate(
        [ref[pl.ds(j, n, stride=PITCH), :] for j in range(ROW_TILE)], axis=1)


def _sigmoid(x):
    return 0.5 * jnp.tanh(0.5 * x) + 0.5


def _inproj_kernel(x_ref, lng_ref, lnb_ref, win_ref, bgate_ref, rc_ref, rs1_ref, rs2_ref,
                   cw_ref, cb_ref, wrg_ref, ba_ref, bx_ref, lam_ref, iconv_ref, ih_ref,
                   q_ref, k_ref, v_ref, rnn_ref, g_ref, conv_ref, h_ref,
                   xr_s, gr_s, xbuf, abuf, ubuf, hcar, *, seg, nseg, tiles_per_stream):
    i = pl.program_id(0)
    slot_proj = i % 2
    slot_rec = 1 - slot_proj

    @pl.when(i == 0)
    def _():
        xr_s[...] = jnp.zeros_like(xr_s)
        gr_s[...] = jnp.zeros_like(gr_s)
        xbuf[...] = jnp.zeros_like(xbuf)
        hcar[...] = jnp.zeros_like(hcar)

    if tiles_per_stream > 1:
        @pl.when((i > 0) & ((i - 1) % tiles_per_stream == 0))
        def _():
            xbuf[0:SUBLANE, :] = iconv_ref[0]
            hcar[...] = ih_ref[0]

    c8 = LRU_C * jax.nn.softplus(-lam_ref[...])
    row = lax.broadcasted_iota(jnp.int32, (SUBLANE, D_RNN), 0)
    cw = cw_ref[...]
    for s in range(nseg):
        r0 = s * seg
        if tiles_per_stream == 1:
            xbuf[0:SUBLANE, :] = iconv_ref[s]
            hcar[...] = ih_ref[s]

        xbuf[SUBLANE:SUBLANE + seg, :] = xr_s[slot_rec, r0:r0 + seg, :]
        xc = (xbuf[SUBLANE:SUBLANE + seg, :] * cw[3:4, :]
              + xbuf[SUBLANE - 1:SUBLANE - 1 + seg, :] * cw[2:3, :]
              + xbuf[SUBLANE - 2:SUBLANE - 2 + seg, :] * cw[1:2, :]
              + xbuf[SUBLANE - 3:SUBLANE - 3 + seg, :] * cw[0:1, :]
              + cb_ref[...])
        tail = xbuf[seg:seg + SUBLANE, :]
        conv_ref[s] = tail
        xbuf[0:SUBLANE, :] = tail

        xcb = xc.astype(BF16)
        blocks = [slice(n * RNN_BLOCK, (n + 1) * RNN_BLOCK) for n in range(N_RNN_BLOCKS)]
        gates = [_dot(xcb[:, sl], wrg_ref[n]) for n, sl in enumerate(blocks)]
        for sl, ri in zip(blocks, gates):
            xcn = xc[:, sl]
            r = _sigmoid(ri[:, :RNN_BLOCK] + ba_ref[:, sl])
            ig = _sigmoid(ri[:, RNN_BLOCK:] + bx_ref[:, sl])
            log_a = -c8[:, sl] * r
            a = jnp.exp(log_a)
            y = -jnp.tanh(log_a) * (a * a + 1.0)
            mult = jnp.where(y > 0.0, y * lax.rsqrt(y), 0.0)
            abuf[0:seg, sl] = a
            ubuf[0:seg, sl] = mult * ig * xcn

        h = hcar[0:1, :]
        for gi in range(seg // SUBLANE):
            o = gi * SUBLANE
            a = abuf[o:o + SUBLANE, :]
            u = ubuf[o:o + SUBLANE, :]
            for sh in (1, 2, 4):
                keep = row >= sh
                u = u + a * jnp.where(keep, pltpu.roll(u, sh, 0), 0.0)
                a = a * jnp.where(keep, pltpu.roll(a, sh, 0), 1.0)
            hg = a * h + u
            ubuf[o:o + SUBLANE, :] = hg
            h = hg[SUBLANE - 1:SUBLANE, :]
        h_b = jnp.broadcast_to(h, (SUBLANE, D_RNN))
        hcar[...] = h_b
        h_ref[s] = h_b
        rnn_ref[r0:r0 + seg, :] = (
            ubuf[0:seg, :] * jax.nn.gelu(gr_s[slot_rec, r0:r0 + seg, :])).astype(BF16)

    zn = _layer_norm(x_ref[...], lng_ref[...], lnb_ref[...]).astype(BF16)
    qkv = _dot(zn, win_ref[:, C_Q:C_XR])
    rc, rs1, rs2 = rc_ref[...], rs1_ref[...], rs2_ref[...]

    def rope(t):
        return t * rc + pltpu.roll(t, 8, 1) * rs1 + pltpu.roll(t, LANE - 8, 1) * rs2

    for j in range(D_ATTN // LANE):
        q_ref[:, j * LANE:(j + 1) * LANE] = (
            rope(qkv[:, j * LANE:(j + 1) * LANE]) * (HEAD_DIM ** -0.5)).astype(BF16)
    k_ref[...] = rope(qkv[:, C_K:C_V])
    v_ref[...] = qkv[:, C_V:C_XR]
    g_ref[...] = _sigmoid(_dot(zn, win_ref[:, C_GL:D_IN]) + bgate_ref[...]).astype(BF16)
    xr_s[slot_proj] = _dot(zn, win_ref[:, C_XR:C_GR])
    gr_s[slot_proj] = _dot(zn, win_ref[:, C_GR:C_GL])


def _inproj_call(x2d, rope_tabs, iconv, ih, wts, *, tm, seg, tiles_per_stream, rope_tiles, name):
    n = x2d.shape[0]
    nt = n // tm
    nseg = tm // seg
    n_streams = n // (seg * tiles_per_stream) if tiles_per_stream > 1 else n // seg
    shared_init = iconv.shape[0] == 1

    def cur(i):
        return jnp.minimum(i, nt - 1)

    def prev(i):
        return jnp.maximum(i - 1, 0)

    def full(a):
        return pl.BlockSpec(a.shape, lambda i: (0,) * a.ndim)

    def rows(w):
        return pl.BlockSpec((tm, w), lambda i: (cur(i), 0))

    rope_spec = pl.BlockSpec((tm, LANE), lambda i: (cur(i) % rope_tiles, 0))
    if shared_init:
        init_spec = pl.BlockSpec((1, SUBLANE, D_RNN), lambda i: (0, 0, 0))
    else:
        init_spec = pl.BlockSpec((nseg, SUBLANE, D_RNN), lambda i: (prev(i), 0, 0))
    state_spec = pl.BlockSpec((nseg, SUBLANE, D_RNN),
                              lambda i: (prev(i) // tiles_per_stream, 0, 0))

    (lng, lnb, win, bgate, cw, cb, wrg, ba, bx, lam) = wts
    kern = functools.partial(_inproj_kernel, seg=seg, nseg=nseg, tiles_per_stream=tiles_per_stream)
    return pl.pallas_call(
        kern,
        grid=(nt + 1,),
        in_specs=[rows(D_MODEL), full(lng), full(lnb), full(win), full(bgate),
                  rope_spec, rope_spec, rope_spec,
                  full(cw), full(cb), full(wrg), full(ba), full(bx), full(lam),
                  init_spec, init_spec],
        out_specs=[rows(D_ATTN), rows(D_KV), rows(D_KV),
                   pl.BlockSpec((tm, D_RNN), lambda i: (prev(i), 0)),
                   rows(2 * D_MODEL), state_spec, state_spec],
        out_shape=[jax.ShapeDtypeStruct((n, D_ATTN), BF16),
                   jax.ShapeDtypeStruct((n, D_KV), F32),
                   jax.ShapeDtypeStruct((n, D_KV), F32),
                   jax.ShapeDtypeStruct((n, D_RNN), BF16),
                   jax.ShapeDtypeStruct((n, 2 * D_MODEL), BF16),
                   jax.ShapeDtypeStruct((n_streams, SUBLANE, D_RNN), F32),
                   jax.ShapeDtypeStruct((n_streams, SUBLANE, D_RNN), F32)],
        scratch_shapes=[pltpu.VMEM((2, tm, D_RNN), F32),
                        pltpu.VMEM((2, tm, D_RNN), F32),
                        pltpu.VMEM((SUBLANE + seg, D_RNN), F32),
                        pltpu.VMEM((seg, D_RNN), F32),
                        pltpu.VMEM((seg, D_RNN), F32),
                        pltpu.VMEM((SUBLANE, D_RNN), F32)],
        compiler_params=_params(),
        name=name,
    )(x2d, lng, lnb, win, bgate, *rope_tabs, cw, cb, wrg, ba, bx, lam, iconv, ih)


HIST = (CONV_W - 1) * SUBLANE
N_SLAB = D_RNN // LANE


def _inproj_tm_kernel(x_ref, lng_ref, lnb_ref, win_ref, bgate_ref, rc_ref, rs1_ref, rs2_ref,
                      cw_ref, cb_ref, wrg_ref, ba_ref, bx_ref, lam_ref, ihist_ref, ih_ref,
                      *rest, tt, n_side):
    side_in = rest[:n_side]
    q_ref, k_ref, v_ref, rnn_ref, g_ref, hist_ref, h_ref = rest[n_side:n_side + 7]
    side_out = rest[n_side + 7:2 * n_side + 7]
    xt_s, gr_s, abuf, ubuf, ibuf, hist, hcar = rest[2 * n_side + 7:]
    for src, dst in zip(side_in, side_out):
        dst[...] = src[...].astype(BF16)
    nb = SUBLANE
    rows = nb * tt
    i = pl.program_id(0)
    slot_proj = i % 2
    slot_rec = 1 - slot_proj

    @pl.when(i == 0)
    def _():
        xt_s[...] = jnp.zeros_like(xt_s)
        gr_s[...] = jnp.zeros_like(gr_s)
        hist[...] = jnp.zeros_like(hist)
        hcar[...] = jnp.zeros_like(hcar)

    @pl.when(i == 1)
    def _():
        hist[...] = ihist_ref[...]
        hcar[...] = ih_ref[...]

    c8 = LRU_C * jax.nn.softplus(-lam_ref[...])
    cw = cw_ref[...]
    for j in range(N_SLAB):
        sl = slice(j * LANE, (j + 1) * LANE)
        xt = xt_s.at[slot_rec, j]
        xt[0:HIST, :] = hist[j]
        xc = (xt[HIST:HIST + rows, :] * cw[3:4, sl]
              + xt[HIST - nb:HIST - nb + rows, :] * cw[2:3, sl]
              + xt[HIST - 2 * nb:HIST - 2 * nb + rows, :] * cw[1:2, sl]
              + xt[0:rows, :] * cw[0:1, sl]
              + cb_ref[:, sl])
        hist[j] = xt[rows:rows + HIST, :]
        ri = _dot(xc.astype(BF16), wrg_ref[j])
        ubuf[j] = xc
        abuf[j] = ri[:, :RNN_BLOCK]
        ibuf[j] = ri[:, RNN_BLOCK:]
    gates_done = jnp.minimum(jnp.abs(ri[0:1, 0:RNN_BLOCK]), 0.0)
    for j in range(N_SLAB):
        sl = slice(j * LANE, (j + 1) * LANE)
        xc = ubuf[j]
        r = _sigmoid(abuf[j] + (ba_ref[:, sl] + gates_done))
        ig = _sigmoid(ibuf[j] + (bx_ref[:, sl] + gates_done))
        log_a = -c8[:, sl] * r
        a = jnp.exp(log_a)
        y = -jnp.tanh(log_a) * (a * a + 1.0)
        mult = jnp.where(y > 0.0, y * lax.rsqrt(y), 0.0)
        abuf[j] = a
        ubuf[j] = mult * ig * xc

    for j in range(N_SLAB):
        h = hcar[j]
        for t in range(tt):
            h = abuf[j, t * nb:(t + 1) * nb, :] * h + ubuf[j, t * nb:(t + 1) * nb, :]
            ubuf[j, t * nb:(t + 1) * nb, :] = h
        hcar[j] = h
    h_ref[...] = hcar[...]
    hist_ref[...] = hist[...]

    for b in range(nb):
        for j in range(N_SLAB):
            sl = slice(j * LANE, (j + 1) * LANE)
            hb = ubuf.at[j][pl.ds(b, tt, stride=nb), :]
            gate = jax.nn.gelu(gr_s[slot_rec, b * tt:(b + 1) * tt, sl])
            rnn_ref[b, :, sl] = (hb * gate).astype(BF16)

    zn = _layer_norm(x_ref[...].reshape(rows, D_MODEL), lng_ref[...], lnb_ref[...]).astype(BF16)
    qkv = _dot(zn, win_ref[:, C_Q:C_XR])
    rc, rs1, rs2 = (jnp.concatenate([t_ref[...]] * nb, axis=0) for t_ref in (rc_ref, rs1_ref, rs2_ref))

    def rope(t):
        return t * rc + pltpu.roll(t, 8, 1) * rs1 + pltpu.roll(t, LANE - 8, 1) * rs2

    qs = [(rope(qkv[:, j * LANE:(j + 1) * LANE]) * (HEAD_DIM ** -0.5)).astype(BF16)
          for j in range(D_ATTN // LANE)]
    kk = rope(qkv[:, C_K:C_V])
    g = _sigmoid(_dot(zn, win_ref[:, C_GL:D_IN]) + bgate_ref[...]).astype(BF16)
    xr = _dot(zn, win_ref[:, C_XR:C_GR])
    gr_s[slot_proj] = _dot(zn, win_ref[:, C_GR:C_GL])
    for b in range(nb):
        rb = slice(b * tt, (b + 1) * tt)
        for j, qj in enumerate(qs):
            q_ref[b, :, j * LANE:(j + 1) * LANE] = qj[rb, :]
        k_ref[b] = kk[rb, :]
        v_ref[b] = qkv[rb, C_V:C_XR]
        g_ref[b] = g[rb, :]
        for j in range(N_SLAB):
            xt_s.at[slot_proj, j][pl.ds(HIST + b, tt, stride=nb), :] = xr[rb, j * LANE:(j + 1) * LANE]


def _inproj_tm_call(x3d, rope_tabs, ihist, ih, wts, expert_w, *, tt, name):
    nb, s, _ = x3d.shape
    assert nb == SUBLANE and s % tt == 0 and tt % (2 * SUBLANE) == 0
    nt = s // tt
    rows = nb * tt
    assert all(w.shape[0] % (nt * 2 * SUBLANE) == 0 for w in expert_w)

    def cur(i):
        return jnp.minimum(i, nt - 1)

    def prev(i):
        return jnp.maximum(i - 1, 0)

    def full(a):
        return pl.BlockSpec(a.shape, lambda i: (0,) * a.ndim)

    def const(a):
        return pl.BlockSpec(a.shape, lambda i: (0,) * a.ndim, pipeline_mode=pl.Buffered(1))

    def tile(w, which=cur):
        return pl.BlockSpec((nb, tt, w), lambda i: (0, which(i), 0))

    rope_spec = pl.BlockSpec((tt, LANE), lambda i: (cur(i), 0))
    ew_specs = [pl.BlockSpec((w.shape[0] // nt, w.shape[1]), lambda i: (cur(i), 0)) for w in expert_w]
    (lng, lnb, win, bgate, cw, cb, wrg, ba, bx, lam) = wts
    return pl.pallas_call(
        functools.partial(_inproj_tm_kernel, tt=tt, n_side=len(expert_w)),
        grid=(nt + 1,),
        in_specs=[tile(D_MODEL), full(lng), full(lnb), const(win), full(bgate),
                  rope_spec, rope_spec, rope_spec,
                  full(cw), full(cb), const(wrg), full(ba), full(bx), full(lam),
                  full(ihist), full(ih)] + ew_specs,
        out_specs=[tile(D_ATTN), tile(D_KV), tile(D_KV), tile(D_RNN, prev), tile(2 * D_MODEL),
                   full(ihist), full(ih)] + ew_specs,
        out_shape=[jax.ShapeDtypeStruct((nb, s, D_ATTN), BF16),
                   jax.ShapeDtypeStruct((nb, s, D_KV), F32),
                   jax.ShapeDtypeStruct((nb, s, D_KV), F32),
                   jax.ShapeDtypeStruct((nb, s, D_RNN), BF16),
                   jax.ShapeDtypeStruct((nb, s, 2 * D_MODEL), BF16),
                   jax.ShapeDtypeStruct(ihist.shape, F32),
                   jax.ShapeDtypeStruct(ih.shape, F32)]
                  + [jax.ShapeDtypeStruct(w.shape, BF16) for w in expert_w],
        scratch_shapes=[pltpu.VMEM((2, N_SLAB, HIST + rows, LANE), F32),
                        pltpu.VMEM((2, rows, D_RNN), F32),
                        pltpu.VMEM((N_SLAB, rows, LANE), F32),
                        pltpu.VMEM((N_SLAB, rows, LANE), F32),
                        pltpu.VMEM((N_SLAB, rows, LANE), F32),
                        pltpu.VMEM((N_SLAB, HIST, LANE), F32),
                        pltpu.VMEM((N_SLAB, SUBLANE, LANE), F32)],
        compiler_params=_params(),
        name=name,
    )(x3d, lng, lnb, win, bgate, *rope_tabs, cw, cb, wrg, ba, bx, lam, ihist, ih, *expert_w)


N_KEYS = N_META + WINDOW + CHUNK
KEY_PAD = 2 * LANE


def _attn_kernel(sink_ref, q_ref, km_ref, vm_ref, kp_ref, vp_ref, ko_ref, vo_ref, o_ref,
                 *, nchunk, tiles_per_stream, mask_first, separate):
    i = pl.program_id(0)
    first = (i % tiles_per_stream) == 0
    lane = lax.broadcasted_iota(jnp.int32, (1, LANE), 1)
    lo = lane < HEAD_DIM

    def slabs(x):
        xs = pltpu.roll(x, HEAD_DIM, 1)
        return (jnp.where(lo, x, xs).astype(BF16), jnp.where(lo, xs, x).astype(BF16))

    km, vm = slabs(km_ref[...]), slabs(vm_ref[...])
    kp, vp = slabs(kp_ref[...]), slabs(vp_ref[...])
    ko, vo = slabs(ko_ref[...]), slabs(vo_ref[...])
    zpad = jnp.zeros((KEY_PAD - N_KEYS, LANE), BF16)
    kpos = lax.broadcasted_iota(jnp.int32, (1, KEY_PAD), 1)
    qrow = lax.broadcasted_iota(jnp.int32, (4 * CHUNK, 1), 0)
    q_per_kv = N_HEADS // N_KV

    def keys(meta, prev, own, c):
        if separate:
            return [meta[N_META * c:N_META * (c + 1)], prev[WINDOW * c:WINDOW * (c + 1)],
                    own[CHUNK * c:CHUNK * (c + 1)]]
        parts = [meta]
        if CHUNK * c < WINDOW:
            parts.append(prev[CHUNK * c:WINDOW])
        parts.append(own[max(CHUNK * c - WINDOW, 0):CHUNK * (c + 1)])
        return parts

    for c in range(nchunk):
        qc = q_ref[c * CHUNK:(c + 1) * CHUNK, :]
        for kv in range(N_KV):
            kc = jnp.concatenate(keys(km[kv], kp[kv], ko[kv], c) + [zpad], axis=0)
            vc = jnp.concatenate(keys(vm[kv], vp[kv], vo[kv], c) + [zpad], axis=0)
            qs = []
            for g in range(q_per_kv):
                h = kv * q_per_kv + g
                slab = qc[:, (h // 2) * LANE:(h // 2 + 1) * LANE]
                keep = lo if h % 2 == 0 else jnp.logical_not(lo)
                qs.append(jnp.where(keep, slab, jnp.zeros_like(slab)))
            qst = jnp.concatenate(qs, axis=0)
            s = lax.dot_general(qst, kc, (((1,), (1,)), ((), ())), preferred_element_type=F32)
            bad = kpos >= N_KEYS
            n_masked = WINDOW - CHUNK * c
            if mask_first and n_masked > 0:
                lim = N_META + jnp.where(first, n_masked, 0)
                bad = bad | ((kpos >= N_META) & (kpos < lim))
            s = jnp.where(bad, -jnp.inf, s)
            sink = jnp.full((4 * CHUNK, 1), sink_ref[kv * q_per_kv + q_per_kv - 1], F32)
            for g in range(q_per_kv - 2, -1, -1):
                sink = jnp.where(qrow < (g + 1) * CHUNK, sink_ref[kv * q_per_kv + g], sink)
            m = jnp.maximum(jnp.max(s, axis=-1, keepdims=True), sink)
            p = jnp.exp(s - m)
            denom = jnp.sum(p, axis=-1, keepdims=True) + jnp.exp(sink - m)
            o = _dot(p.astype(BF16), vc) / denom
            for jj in range(q_per_kv // 2):
                g0 = 2 * jj
                oslab = jnp.where(lo, o[g0 * CHUNK:(g0 + 1) * CHUNK, :],
                                  o[(g0 + 1) * CHUNK:(g0 + 2) * CHUNK, :])
                j = kv * (q_per_kv // 2) + jj
                o_ref[c * CHUNK:(c + 1) * CHUNK, j * LANE:(j + 1) * LANE] = oslab.astype(BF16)


def _attn_call(sinks, q, kmeta, vmeta, kprev, vprev, kown, vown, *, tq, tiles_per_stream,
               mask_first, prev_map, meta_map, name, separate=False):
    n = q.shape[0]
    nt = n // tq
    nchunk = tq // CHUNK
    kern = functools.partial(_attn_kernel, nchunk=nchunk, tiles_per_stream=tiles_per_stream,
                             mask_first=mask_first, separate=separate)
    per_tile = nchunk if separate else 1
    meta_spec = pl.BlockSpec((per_tile * N_META, D_KV), meta_map)
    prev_spec = pl.BlockSpec((per_tile * WINDOW, D_KV), prev_map)
    own_spec = pl.BlockSpec((tq, D_KV), lambda i: (i, 0))
    return pl.pallas_call(
        kern,
        grid=(nt,),
        in_specs=[pl.BlockSpec(memory_space=pltpu.SMEM),
                  pl.BlockSpec((tq, D_ATTN), lambda i: (i, 0)),
                  meta_spec, meta_spec, prev_spec, prev_spec, own_spec, own_spec],
        out_specs=pl.BlockSpec((tq, D_ATTN), lambda i: (i, 0)),
        out_shape=jax.ShapeDtypeStruct((n, D_ATTN), BF16),
        compiler_params=_params(),
        name=name,
    )(sinks, q, kmeta, vmeta, kprev, vprev, kown, vown)


def _merge_kernel(o_ref, rnn_ref, g_ref, x_ref, lng_ref, lnb_ref, wpa_ref, wpr_ref, wo_ref,
                  l1g_ref, l1b_ref, wr_ref, br_ref, triu_ref, cin_ref,
                  z1_ref, z1t_ref, slot_ref, cout_ref, cnt):
    i = pl.program_id(0)

    @pl.when(i == 0)
    def _():
        cnt[...] = jnp.concatenate([cin_ref[...]] * (cnt.shape[1] // LANE), axis=1)

    g = g_ref[...]
    pa = _dot(o_ref[...], wpa_ref[...])
    pr = _dot(rnn_ref[...], wpr_ref[...])
    mixed = g[:, :D_MODEL].astype(F32) * pa + g[:, D_MODEL:].astype(F32) * pr
    mix = _dot(mixed.astype(BF16), wo_ref[...])
    zn = _layer_norm(x_ref[...], lng_ref[...], lnb_ref[...])
    z1 = _layer_norm(ALPHA * zn + mix, l1g_ref[...], l1b_ref[...])
    z1_ref[...] = z1

    z_hi = z1.astype(BF16)
    z_lo = (z1 - z_hi.astype(F32)).astype(BF16)
    la = _dot(z_hi, wr_ref[...])
    lb = _dot(z_lo, wr_ref[...])
    logit = la[:, :LANE] + la[:, LANE:] + lb[:, :LANE] + lb[:, LANE:] + br_ref[...]

    tm = logit.shape[0]
    lt = logit.T[0:ROUTE_ROWS, :]
    row = lax.broadcasted_iota(jnp.int32, (ROUTE_ROWS, tm), 0)
    rin = (row & 3).astype(F32)
    seg = (row >> 2).astype(F32)
    odd1 = (row & 1) == 1
    odd2 = (row & 2) == 2
    ninf = -jnp.inf

    def seg_reduce(x, op):
        y = op(x, jnp.where(odd1, pltpu.roll(x, 1, 0), pltpu.roll(x, ROUTE_ROWS - 1, 0)))
        return op(y, jnp.where(odd2, pltpu.roll(y, 2, 0), pltpu.roll(y, ROUTE_ROWS - 2, 0)))

    xl = jnp.where(row < N_GROUPS * (1 + EXPERTS_PER_GROUP), lt, ninf)
    m1 = seg_reduce(xl, jnp.maximum)
    i1 = seg_reduce(jnp.where(xl == m1, rin, 4.0), jnp.minimum)
    xl2 = jnp.where(rin == i1, ninf, xl)
    m2 = seg_reduce(xl2, jnp.maximum)
    i2 = seg_reduce(jnp.where(xl2 == m2, rin, 4.0), jnp.minimum)
    esum = seg_reduce(jnp.where(row < N_GROUPS, jnp.exp(xl - m1), 0.0), jnp.add)
    gidx = i1[0:1, :]
    gw = 1.0 / esum[0:1, :]
    chosen = seg == gidx + 1.0

    def pick(x):
        return jnp.max(jnp.where(chosen, x, ninf), axis=0, keepdims=True)

    v1, v2, l1, l2 = pick(m1), pick(m2), pick(i1), pick(i2)
    e21 = jnp.exp(v2 - v1)
    w1 = gw / (1.0 + e21)
    w2 = gw * e21 / (1.0 + e21)
    first_lo = l1 < l2
    ea, eb = jnp.minimum(l1, l2), jnp.maximum(l1, l2)
    w_lo = jnp.where(first_lo, w1, w2)
    w_hi = jnp.where(first_lo, w2, w1)
    lex = ea * (7.0 - ea) * 0.5 + (eb - ea - 1.0)
    pair = jnp.where(lex == 3.0, 4.0, jnp.where(lex == 4.0, 3.0, lex))
    flipped = lex == 5.0
    wa = jnp.where(flipped, w_hi, w_lo)
    wb = jnp.where(flipped, w_lo, w_hi)
    cls = gidx * N_PAIR + pair

    onehot = row.astype(F32) == cls
    ohf = jnp.where(onehot, 1.0, 0.0)
    before = _dot(ohf.astype(BF16), triu_ref[...])
    rank = jnp.sum(jnp.where(onehot, before + cnt[...], 0.0), axis=0, keepdims=True)
    cnt[...] = cnt[...] + jnp.sum(ohf, axis=1, keepdims=True)
    cout_ref[...] = cnt[:, 0:LANE]
    slot_ref[0] = jnp.broadcast_to(cls, (SUBLANE, tm))
    slot_ref[1] = jnp.broadcast_to(rank, (SUBLANE, tm))

    srow = lax.broadcasted_iota(jnp.int32, (LANE, tm), 0)
    scalars = jnp.where(srow == 1, wa, jnp.where(srow == 2, wb, 0.0)).T
    _store_token_rows(z1t_ref, z1, scalars)


def _merge_call(o, rnn, g, x2d, counts_in, wts, *, name):
    n = x2d.shape[0]
    nt = n // TM

    def full(a):
        return pl.BlockSpec(a.shape, lambda i: (0,) * a.ndim)

    def rows(w):
        return pl.BlockSpec((TM, w), lambda i: (i, 0))

    return pl.pallas_call(
        _merge_kernel,
        grid=(nt,),
        in_specs=[rows(D_ATTN), rows(D_RNN), rows(2 * D_MODEL), rows(D_MODEL)]
                 + [full(w) for w in wts] + [full(counts_in)],
        out_specs=[rows(D_MODEL), pl.BlockSpec((TM * PITCH, LANE), lambda i: (i, 0)),
                   pl.BlockSpec((None, 2, SUBLANE, TM), lambda i: (i, 0, 0, 0)),
                   pl.BlockSpec((ROUTE_ROWS, LANE), lambda i: (0, 0))],
        out_shape=[jax.ShapeDtypeStruct((n, D_MODEL), F32),
                   jax.ShapeDtypeStruct((n * PITCH, LANE), F32),
                   jax.ShapeDtypeStruct((nt, 2, SUBLANE, TM), F32),
                   jax.ShapeDtypeStruct((ROUTE_ROWS, LANE), F32)],
        scratch_shapes=[pltpu.VMEM((ROUTE_ROWS, TM), F32)],
        compiler_params=_params(),
        name=name,
    )(o, rnn, g, x2d, *wts, counts_in)


def _plan_kernel(cnt_ref, slot_p_ref, slot_s_ref, pos_p_ref, pos_s_ref, tiles_ref):
    shift = TS.bit_length() - 1
    off, end, fill, fill_p = [], [], [], []
    run = jnp.int32(0)
    last = jnp.int32(0)
    for c in range(N_CLASS):
        n = cnt_ref[c]
        off.append(run)
        fill.append(run + n)
        fill_p.append(run + cnt_ref[N_CLASS + c])
        run = run + lax.shift_left(lax.shift_right_logical(n + (TS - 1), shift), shift)
        end.append(run)
        last = jnp.where(n > 0, c, last)

    def slots(slot_ref, pos_ref):
        cls = slot_ref[:, 0]
        o = jnp.zeros(cls.shape, jnp.int32)
        for c in range(N_CLASS):
            o = jnp.where(cls == float(c), off[c], o)
        pos_ref[...] = ((o + slot_ref[:, 1].astype(jnp.int32)) * PITCH)[:, 0:1, :]

    slots(slot_p_ref, pos_p_ref)
    slots(slot_s_ref, pos_s_ref)

    tstart = lax.broadcasted_iota(jnp.int32, (SUBLANE, LANE), 1) * TS
    tcls = jnp.zeros((SUBLANE, LANE), jnp.int32)
    for c in range(N_CLASS):
        tcls = tcls + jnp.where(tstart >= end[c], 1, 0)
    tcls = jnp.minimum(tcls, last)
    zero = jnp.zeros((SUBLANE, LANE), jnp.int32)
    grp, ea, eb, rows, rows_p = zero, zero, zero, zero, zero
    for c in range(N_CLASS):
        sel = tcls == c
        first = (c // N_PAIR) * EXPERTS_PER_GROUP
        grp = jnp.where(sel, c // N_PAIR, grp)
        ea = jnp.where(sel, first + int(PAIR_A[c % N_PAIR]), ea)
        eb = jnp.where(sel, first + int(PAIR_B[c % N_PAIR]), eb)
        rows = jnp.where(sel, fill[c], rows)
        rows_p = jnp.where(sel, fill_p[c], rows_p)
    rows = jnp.clip(rows - tstart, 0, TS)
    rows_p = jnp.clip(rows_p - tstart, 0, TS)
    sub = lax.broadcasted_iota(jnp.int32, (SUBLANE, LANE), 0)
    tiles_ref[...] = jnp.where(sub == 0, grp, jnp.where(sub == 1, ea, jnp.where(
        sub == 2, eb, jnp.where(sub == 3, rows, rows_p))))


def _plan_call(counts, slot_p, slot_s):
    def full(a):
        return pl.BlockSpec(a.shape, lambda i: (0,) * a.ndim)

    pos_shapes = [jax.ShapeDtypeStruct((a.shape[0], 1, TM), jnp.int32) for a in (slot_p, slot_s)]
    tiles_shape = jax.ShapeDtypeStruct((SUBLANE, LANE), jnp.int32)
    return pl.pallas_call(
        _plan_kernel,
        grid=(1,),
        in_specs=[pl.BlockSpec(memory_space=pltpu.SMEM), full(slot_p), full(slot_s)],
        out_specs=[full(s) for s in pos_shapes] + [full(tiles_shape)],
        out_shape=pos_shapes + [tiles_shape],
        compiler_params=_params(),
        name="slot_plan",
    )(counts, slot_p, slot_s)


def _dispatch_kernel(pos_ref, extra_ref, z1t_ref, xs_ref, rowbuf, sem, *, nt, fresh):
    i = pl.program_id(0)
    slot = i % 2
    tile_rows = TS * PITCH

    def row_copy(sl, r, p):
        return pltpu.make_async_copy(rowbuf.at[sl, pl.ds(r * PITCH, PITCH), :],
                                     xs_ref.at[pl.ds(p, PITCH), :], sem.at[sl])

    def drain(sl):
        pltpu.make_async_copy(rowbuf.at[sl], xs_ref.at[pl.ds(0, TD * PITCH), :], sem.at[sl]).wait()

    if fresh:
        @pl.when(i == 0)
        def _():
            rowbuf[1, 0:tile_rows, :] = jnp.zeros((tile_rows, LANE), F32)
            n_tiles = extra_ref.shape[0]

            def fill_copy(t):
                dst = pl.multiple_of(t * tile_rows, tile_rows)
                return pltpu.make_async_copy(rowbuf.at[1, pl.ds(0, tile_rows), :],
                                             xs_ref.at[pl.ds(dst, tile_rows), :], sem.at[2])

            def fill(t, c):
                @pl.when(extra_ref[t] < TS)
                def _():
                    fill_copy(t).start()
                return c
            lax.fori_loop(0, n_tiles, fill, 0)

            def fill_wait(t, c):
                @pl.when(extra_ref[t] < TS)
                def _():
                    fill_copy(0).wait()
                return c
            lax.fori_loop(0, n_tiles, fill_wait, 0)

    @pl.when(i >= 2)
    def _():
        drain(slot)

    rowbuf[slot] = z1t_ref[...]

    def issue(rb, c):
        for u in range(DMA_UNROLL):
            r = rb * DMA_UNROLL + u
            row_copy(slot, r, pos_ref[0, r]).start(priority=u % 2)
        return c
    lax.fori_loop(0, TD // DMA_UNROLL, issue, 0)

    @pl.when(i == nt - 1)
    def _():
        drain(slot)
        if nt > 1:
            drain(1 - slot)


def _dispatch_call(pos3, z1t, xs, tile_rows, *, name):
    nt = z1t.shape[0] // (TD * PITCH)
    fresh = xs is None
    n_slots = tile_rows.shape[0] * TS if fresh else xs.shape[0] // PITCH
    extra = tile_rows if fresh else xs
    extra_spec = pl.BlockSpec(memory_space=pltpu.SMEM if fresh else pl.ANY)
    return pl.pallas_call(
        functools.partial(_dispatch_kernel, nt=nt, fresh=fresh),
        grid=(nt,),
        in_specs=[pl.BlockSpec((None, 1, TD), lambda i: (i, 0, 0), memory_space=pltpu.SMEM),
                  extra_spec,
                  pl.BlockSpec((TD * PITCH, LANE), lambda i: (i, 0))],
        out_specs=pl.BlockSpec(memory_space=pl.ANY),
        out_shape=jax.ShapeDtypeStruct((n_slots * PITCH, LANE), F32),
        input_output_aliases={} if fresh else {1: 0},
        scratch_shapes=[pltpu.VMEM((2, TD * PITCH, LANE), F32), pltpu.SemaphoreType.DMA((3,))],
        compiler_params=_params(),
        name=name,
    )(pos3, extra, z1t)


def _moe_kernel(grp_ref, ea_ref, eb_ref, val_ref, xs_ref, wg_ref, wu_ref, wd_ref, ys_ref):
    del grp_ref
    i = pl.program_id(0)

    @pl.when(val_ref[i] > 0)
    def _():
        xb = _load_token_rows(xs_ref, TS).astype(BF16)
        scalars = xs_ref[pl.ds(ROW_TILE, TS, stride=PITCH), :]
        ea = ea_ref[i] % EXPERTS_PER_GROUP
        eb = eb_ref[i] % EXPERTS_PER_GROUP
        ga, ua = _dot(xb, wg_ref[ea]), _dot(xb, wu_ref[ea])
        gb, ub = _dot(xb, wg_ref[eb]), _dot(xb, wu_ref[eb])
        ya = _dot((jax.nn.silu(ga) * ua).astype(BF16), wd_ref[ea])
        yb = _dot((jax.nn.silu(gb) * ub).astype(BF16), wd_ref[eb])
        y = scalars[:, 1:2] * ya + scalars[:, 2:3] * yb
        _store_token_rows(ys_ref, y, jnp.zeros((TS, LANE), F32))

    @pl.when(val_ref[i] == 0)
    def _():
        ys_ref[...] = jnp.zeros_like(ys_ref)


def _moe_call(tile_grp, tile_ea, tile_eb, tile_valid, xs, wg, wu, wd):
    n_tiles = xs.shape[0] // (TS * PITCH)
    wspec_in = pl.BlockSpec((None, EXPERTS_PER_GROUP, D_MODEL, D_EXPERT),
                            lambda i, grp, *_: (grp[i], 0, 0, 0))
    wspec_out = pl.BlockSpec((None, EXPERTS_PER_GROUP, D_EXPERT, D_MODEL),
                             lambda i, grp, *_: (grp[i], 0, 0, 0))
    grid_spec = pltpu.PrefetchScalarGridSpec(
        num_scalar_prefetch=4,
        grid=(n_tiles,),
        in_specs=[pl.BlockSpec((TS * PITCH, LANE), lambda i, *_: (i, 0)),
                  wspec_in, wspec_in, wspec_out],
        out_specs=pl.BlockSpec((TS * PITCH, LANE), lambda i, *_: (i, 0)),
    )
    return pl.pallas_call(
        _moe_kernel,
        grid_spec=grid_spec,
        out_shape=jax.ShapeDtypeStruct((n_tiles * TS * PITCH, LANE), F32),
        compiler_params=_params(),
        name="moe_experts",
    )(tile_grp, tile_ea, tile_eb, tile_valid, xs, wg, wu, wd)


def _combine_kernel(pos_ref, posn_ref, z1_ref, l2g_ref, l2b_ref, ys_ref, out_ref, ybuf, sem, *, nt):
    i = pl.program_id(0)
    slot = i % 2

    def row_copy(sl, r, p):
        return pltpu.make_async_copy(ys_ref.at[pl.ds(p, ROW_TILE), :],
                                     ybuf.at[sl, pl.ds(r * PITCH, ROW_TILE), :], sem.at[sl])

    def issue(sl, pref):
        def body(rb, c):
            for u in range(DMA_UNROLL):
                r = rb * DMA_UNROLL + u
                row_copy(sl, r, pref[0, r]).start(priority=u % 2)
            return c
        lax.fori_loop(0, TM // DMA_UNROLL, body, 0)

    @pl.when(i == 0)
    def _():
        issue(0, pos_ref)

    @pl.when(i + 1 < nt)
    def _():
        issue(1 - slot, posn_ref)

    pltpu.make_async_copy(ys_ref.at[pl.ds(0, TM * ROW_TILE), :],
                          ybuf.at[slot, pl.ds(0, TM * ROW_TILE), :], sem.at[slot]).wait()

    moe = _load_token_rows(ybuf.at[slot], TM)
    out_ref[...] = _layer_norm(ALPHA * z1_ref[...] + moe, l2g_ref[...], l2b_ref[...])


def _combine_call(pos3, z1, l2g, l2b, ys, *, name):
    n = z1.shape[0]
    nt = n // TM
    return pl.pallas_call(
        functools.partial(_combine_kernel, nt=nt),
        grid=(nt,),
        in_specs=[pl.BlockSpec((None, 1, TM), lambda i: (i, 0, 0), memory_space=pltpu.SMEM),
                  pl.BlockSpec((None, 1, TM), lambda i: (jnp.minimum(i + 1, nt - 1), 0, 0),
                               memory_space=pltpu.SMEM),
                  pl.BlockSpec((TM, D_MODEL), lambda i: (i, 0)),
                  pl.BlockSpec(l2g.shape, lambda i: (0, 0)),
                  pl.BlockSpec(l2b.shape, lambda i: (0, 0)),
                  pl.BlockSpec(memory_space=pl.ANY)],
        out_specs=pl.BlockSpec((TM, D_MODEL), lambda i: (i, 0)),
        out_shape=jax.ShapeDtypeStruct((n, D_MODEL), F32),
        scratch_shapes=[pltpu.VMEM((2, TM * PITCH, LANE), F32), pltpu.SemaphoreType.DMA((2,))],
        compiler_params=_params(),
        name=name,
    )(pos3, pos3, z1, l2g, l2b, ys)


def _rope_tables(pos, reps=1):
    half = ROPE_DIM // 2
    inv_freq = ROPE_THETA ** (-np.arange(half, dtype=np.float64) / half)
    ang = pos.astype(np.float64)[:, None] * inv_freq[None, :]
    cos, sin = np.cos(ang), np.sin(ang)
    p = pos.shape[0]
    ones = np.ones((p, HEAD_DIM - ROPE_DIM))
    zeros = np.zeros((p, HEAD_DIM - ROPE_DIM))
    zh = np.zeros((p, half))
    c = np.concatenate([cos, cos, ones], axis=1)
    s1 = np.concatenate([zh, sin, zeros], axis=1)
    s2 = np.concatenate([-sin, zh, zeros], axis=1)
    rep = LANE // HEAD_DIM
    return tuple(jnp.asarray(np.tile(t, (reps, rep)).astype(np.float32)) for t in (c, s1, s2))


def kernel(x_prompt, x_sample, cache_meta_k, cache_meta_v, cache_win_k, cache_win_v, state_conv, state_h, meta_tokens, ln_in_g, ln_in_b, w_in, b_gate, attn_sinks, conv_w, conv_b, w_rg_a, b_rg_a, w_rg_x, b_rg_x, lru_lambda, w_branch_attn, w_branch_rnn, w_out, ln1_g, ln1_b, w_group, b_group, w_router, b_router, w_gate, w_up, w_down, ln2_g, ln2_b):
    B, S, _ = x_prompt.shape
    DB, TSQ, _ = x_sample.shape
    assert DEPTH == 1 and w_in.shape == (DEPTH, D_MODEL, D_IN)
    assert S % TM == 0 and (DB * TSQ) % TM == 0 and TM % TSQ == 0 and TSQ == CHUNK
    assert cache_win_k.shape[2] == WINDOW and S >= WINDOW
    l = 0

    def row(v):
        return v.reshape(1, -1).astype(F32)

    lng, lnb = row(ln_in_g), row(ln_in_b)
    win = w_in[l].astype(BF16)
    cw = jnp.concatenate([conv_w[l], jnp.zeros((SUBLANE - CONV_W, D_RNN), F32)], axis=0)
    wrg = jnp.concatenate([w_rg_a[l], w_rg_x[l]], axis=-1).astype(BF16)
    in_wts = (lng, lnb, win, row(b_gate[l]), cw, row(conv_b[l]), wrg,
              row(b_rg_a[l]), row(b_rg_x[l]), row(lru_lambda[l]))
    wr = jnp.concatenate([w_group[l], w_router[l].reshape(D_MODEL, -1)], axis=1)
    wr = jnp.pad(wr, ((0, 0), (0, LANE - wr.shape[1])))
    wr_hi = wr.astype(BF16)
    wr_lo = (wr - wr_hi.astype(F32)).astype(BF16)
    wr_cat = jnp.concatenate([wr_hi, wr_lo], axis=1)
    br = jnp.pad(jnp.concatenate([b_group[l], b_router[l].reshape(-1)]),
                 (0, LANE - N_GROUPS * (1 + EXPERTS_PER_GROUP))).reshape(1, LANE)
    merge_small = (row(ln1_g[l]), row(ln1_b[l]), wr_cat, br,
                   jnp.asarray(np.triu(np.ones((TM, TM), np.float32), 1), BF16))
    sinks = attn_sinks[l].astype(F32)

    rope_meta = _rope_tables(np.arange(N_META))
    rope_p = _rope_tables(N_META + np.arange(S))
    rope_s = _rope_tables(N_META + PAST_LEN + np.arange(TSQ), reps=TM_SAMPLE // TSQ)

    xp = x_prompt.reshape(B * S, D_MODEL)
    xs_ = x_sample.reshape(DB * TSQ, D_MODEL)
    n_p, n_s = B * S, DB * TSQ
    zero_state = jnp.zeros((1, SUBLANE, D_RNN), F32)

    _, k_m, v_m, _, _, conv_m, h_m = _inproj_call(
        meta_tokens.astype(F32), rope_meta, zero_state, zero_state, in_wts,
        tm=N_META, seg=N_META, tiles_per_stream=1, rope_tiles=1, name="inproj_meta")
    n_hist = CONV_W - 1
    ihist = conv_m[0, SUBLANE - n_hist:, :].reshape(n_hist, N_SLAB, LANE).transpose(1, 0, 2)
    ihist = jnp.broadcast_to(ihist[:, :, None, :], (N_SLAB, n_hist, B, LANE)).reshape(N_SLAB, HIST, LANE)
    ih = jnp.broadcast_to(h_m[0, 0].reshape(N_SLAB, 1, LANE), (N_SLAB, B, LANE))
    expert_w = tuple(w[l].astype(F32).reshape(-1, w.shape[-1])
                     for w in (w_gate, w_up, w_down, w_branch_attn, w_branch_rnn, w_out))
    q_p, k_p, v_p, rnn_p, g_p, hist_p, hfin_p, wg, wu, wd, wpa, wpr, wo = (
        a.reshape(n_p, -1) if a.shape[:2] == (B, S) else a
        for a in _inproj_tm_call(x_prompt.astype(F32), rope_p, ihist, ih, in_wts, expert_w,
                                 tt=TM // B, name="inproj_prompt"))
    wg = wg.reshape(N_GROUPS, EXPERTS_PER_GROUP, D_MODEL, D_EXPERT)
    wu = wu.reshape(N_GROUPS, EXPERTS_PER_GROUP, D_MODEL, D_EXPERT)
    wd = wd.reshape(N_GROUPS, EXPERTS_PER_GROUP, D_EXPERT, D_MODEL)
    merge_wts = (lng, lnb, wpa, wpr, wo) + merge_small
    iconv_s =jnp.pad(state_conv[l].astype(F32), ((0, 0), (SUBLANE - (CONV_W - 1), 0), (0, 0)))
    ih_s = jnp.broadcast_to(state_h[l].astype(F32)[:, None, :], (DB, SUBLANE, D_RNN))
    q_s, k_s, v_s, rnn_s, g_s, conv_s, h_s = _inproj_call(
        xs_, rope_s, iconv_s, ih_s, in_wts,
        tm=TM_SAMPLE, seg=TSQ, tiles_per_stream=1, rope_tiles=1, name="inproj_sample")

    tq_p = 4 * TM
    assert S % tq_p == 0
    tps = S // tq_p
    win_blocks = tq_p // WINDOW
    o_p = _attn_call(
        sinks, q_p, k_m, v_m, k_p, v_p, k_p, v_p, tq=tq_p, tiles_per_stream=tps, mask_first=True,
        prev_map=lambda i: (jnp.maximum(i * win_blocks - 1, 0), 0),
        meta_map=lambda i: (0, 0), name="attn_prompt")
    cmk = cache_meta_k[l].reshape(DB * N_META, D_KV).astype(F32)
    cmv = cache_meta_v[l].reshape(DB * N_META, D_KV).astype(F32)
    cwk = cache_win_k[l].reshape(DB * WINDOW, D_KV).astype(F32)
    cwv = cache_win_v[l].reshape(DB * WINDOW, D_KV).astype(F32)
    o_s = _attn_call(
        sinks, q_s, cmk, cmv, cwk, cwv, k_s, v_s, tq=4 * TSQ, tiles_per_stream=1, mask_first=False,
        prev_map=lambda i: (i, 0), meta_map=lambda i: (i, 0), name="attn_sample", separate=True)

    zero_cnt = jnp.zeros((ROUTE_ROWS, LANE), F32)
    z1_p, z1t_p, slot_p, cnt_p = _merge_call(
        o_p, rnn_p, g_p, xp, zero_cnt, merge_wts, name="merge_prompt")
    z1_s, z1t_s, slot_s, cnt_all = _merge_call(
        o_s, rnn_s, g_s, xs_, cnt_p, merge_wts, name="merge_sample")

    n_tiles = (n_p + n_s) // TS + N_CLASS
    assert n_tiles <= LANE
    counts = jnp.concatenate([cnt_all[:N_CLASS, 0], cnt_p[:N_CLASS, 0]]).astype(jnp.int32)
    pos_p, pos_s, tiles = _plan_call(counts, slot_p, slot_s)
    tile_grp, tile_ea, tile_eb, tile_rows, tile_rows_p = (tiles[r, :n_tiles] for r in range(5))

    xs1 = _dispatch_call(pos_p.reshape(-1, 1, TD), z1t_p, None, tile_rows_p, name="dispatch_prompt")
    xs2 = _dispatch_call(pos_s.reshape(-1, 1, TD), z1t_s, xs1, tile_rows, name="dispatch_sample")
    ys = _moe_call(tile_grp, tile_ea, tile_eb, tile_rows, xs2, wg, wu, wd)
    l2g, l2b = row(ln2_g[l]), row(ln2_b[l])
    y_p = _combine_call(pos_p, z1_p, l2g, l2b, ys, name="combine_prompt")
    y_s = _combine_call(pos_s, z1_s, l2g, l2b, ys, name="combine_sample")

    def heads(a, lead):
        return a.reshape(*lead, N_KV, HEAD_DIM)[None]

    tail = slice(SUBLANE - (CONV_W - 1), SUBLANE)
    y_prompt = y_p.reshape(B, S, D_MODEL)
    y_sample = y_s.reshape(DB, TSQ, D_MODEL)
    meta_k_p = jnp.broadcast_to(heads(k_m, (1, N_META)), (1, B, N_META, N_KV, HEAD_DIM))
    meta_v_p = jnp.broadcast_to(heads(v_m, (1, N_META)), (1, B, N_META, N_KV, HEAD_DIM))
    win_k_p = heads(k_p.reshape(B, S, D_KV)[:, S - WINDOW:], (B, WINDOW))
    win_v_p = heads(v_p.reshape(B, S, D_KV)[:, S - WINDOW:], (B, WINDOW))
    conv_out_p = hist_p.reshape(N_SLAB, n_hist, B, LANE).transpose(2, 1, 0, 3).reshape(
        1, B, n_hist, D_RNN)
    h_out_p = hfin_p.transpose(1, 0, 2).reshape(1, B, D_RNN)
    new_k_s = heads(k_s, (DB, TSQ))
    new_v_s = heads(v_s, (DB, TSQ))
    conv_out_s = conv_s[:, tail, :][None]
    h_out_s = h_s[:, 0, :][None].astype(state_h.dtype)
    return (y_prompt, y_sample, meta_k_p, meta_v_p, win_k_p, win_v_p, conv_out_p, h_out_p,
            new_k_s, new_v_s, conv_out_s, h_out_s)
```
